```python
import math
import jax, jax.numpy as jnp
from jax import lax
import numpy as np

D_MODEL = 2048
BATCH = 2
SEQ = 4096
DEPTH = 4
DEC_BATCH = 1
DEC_SEQ = 16384
PAST_LEN = 128

N_ATTN_LAYERS = (DEPTH + 1) // 2
N_RWKV_LAYERS = DEPTH // 2

MLA_HEADS = 8
MLA_Q_RANK = 512
MLA_KV_RANK = 512
MLA_NOPE_DIM = 128
MLA_ROPE_DIM = 64
MLA_V_DIM = 128
ROPE_BASE = 10000.0
Q_BLOCK = 128

DIL_HEADS = 8
DIL_HEAD_DIM = 128
DIL_PATTERNS = ((128, 1), (512, 4), (2048, 16))
DIL_BLOCK = 64

N_BUCKETS = 32
T5_MAX_DISTANCE = 1024

IN_WIDTHS = (MLA_Q_RANK, MLA_KV_RANK, MLA_ROPE_DIM,
             DIL_HEADS * DIL_HEAD_DIM, DIL_HEADS * DIL_HEAD_DIM, DIL_HEADS * DIL_HEAD_DIM)
IN_DIM = sum(IN_WIDTHS)
MIX_WIDTH = MLA_HEADS * MLA_V_DIM + DIL_HEADS * DIL_HEAD_DIM

RWKV_HEAD = 64
RWKV_HEADS = D_MODEL // RWKV_HEAD
RWKV_DECAY_LORA = 96
RWKV_A_LORA = 96
RWKV_GATE_LORA = 256
RWKV_LN_EPS = 64e-5

FFN_HIDDEN = ((8 * D_MODEL + 3 * 256 - 1) // (3 * 256)) * 256

NORM_EPS = 1e-6
NEG_INF = -1e30

kernel_name = "hybrid_mla_dilated_rwkv7_encoder"


def rms_norm(x, g):
    xf = x.astype(jnp.float32)
    y = xf * lax.rsqrt(jnp.mean(xf * xf, axis=-1, keepdims=True) + NORM_EPS)
    return (y * g.astype(jnp.float32)).astype(x.dtype)


def t5_bucket(rel):
    half = N_BUCKETS // 2
    max_exact = half // 2
    bucket = jnp.where(rel > 0, half, 0)
    n = jnp.abs(rel)
    nf = jnp.maximum(n, 1).astype(jnp.float32)
    large = max_exact + (jnp.log(nf / max_exact) / math.log(T5_MAX_DISTANCE / max_exact)
                         * (half - max_exact)).astype(jnp.int32)
    large = jnp.minimum(large, half - 1)
    return bucket + jnp.where(n < max_exact, n, large)


def rope_tables(S):
    inv = ROPE_BASE ** (-jnp.arange(0, MLA_ROPE_DIM, 2, dtype=jnp.float32) / MLA_ROPE_DIM)
    ang = jnp.arange(S, dtype=jnp.float32)[:, None] * inv[None, :]
    return jnp.cos(ang), jnp.sin(ang)


def apply_rope(x, cos, sin):
    half = x.shape[-1] // 2
    x1 = x[..., :half].astype(jnp.float32)
    x2 = x[..., half:].astype(jnp.float32)
    return jnp.concatenate([x1 * cos - x2 * sin, x2 * cos + x1 * sin], axis=-1).astype(x.dtype)


def mla_attention(c_q, c_kv, k_rope, q_norm, kv_norm, w_uq, w_ukv):
    B, S, _ = c_q.shape
    H = MLA_HEADS
    q = (rms_norm(c_q, q_norm) @ w_uq).reshape(B, S, H, MLA_NOPE_DIM + MLA_ROPE_DIM)
    kv = (rms_norm(c_kv, kv_norm) @ w_ukv).reshape(B, S, H, MLA_NOPE_DIM + MLA_V_DIM)
    cos, sin = rope_tables(S)
    q_nope = q[..., :MLA_NOPE_DIM]
    q_rope = apply_rope(q[..., MLA_NOPE_DIM:], cos[:, None, :], sin[:, None, :])
    k_nope = kv[..., :MLA_NOPE_DIM]
    v = kv[..., MLA_NOPE_DIM:]
    k_rope = apply_rope(k_rope, cos, sin)
    scale = (MLA_NOPE_DIM + MLA_ROPE_DIM) ** -0.5
    nq = S // Q_BLOCK

    def to_blocks(t):
        return jnp.swapaxes(t.reshape(B, nq, Q_BLOCK, H, t.shape[-1]), 0, 1)

    def attend(blk):
        qn, qr = blk
        s = (jnp.einsum('bqhd,bkhd->bhqk', qn, k_nope)
             + jnp.einsum('bqhd,bkd->bhqk', qr, k_rope)).astype(jnp.float32) * scale
        p = jax.nn.softmax(s, axis=-1)
        return jnp.einsum('bhqk,bkhd->bqhd', p.astype(v.dtype), v)

    o = lax.map(attend, (to_blocks(q_nope), to_blocks(q_rope)))
    return jnp.swapaxes(o, 0, 1).reshape(B, S, H * MLA_V_DIM)


def dilated_attention(q, k, v, rel_bias):
    Bsz, S, H, dh = q.shape
    BLK = DIL_BLOCK
    scale = dh ** -0.5
    outs, lses = [], []
    for window, dil in DIL_PATTERNS:
        half = window // (2 * dil)
        L = S // dil
        nb = -(-L // BLK)
        Lp = nb * BLK
        G = Bsz * dil

        def to_sub(t):
            return t.reshape(Bsz, L, dil, H, dh).transpose(0, 2, 1, 3, 4).reshape(G, L, H, dh)

        def key_windows(t):
            tp = jnp.pad(to_sub(t), ((0, 0), (BLK, Lp - L + BLK), (0, 0), (0, 0)))
            tp = tp.reshape(G, nb + 2, BLK, H, dh)
            return jnp.concatenate([tp[:, :-2], tp[:, 1:-1], tp[:, 2:]], axis=2)

        qs = jnp.pad(to_sub(q), ((0, 0), (0, Lp - L), (0, 0), (0, 0))).reshape(G, nb, BLK, H, dh)
        kw = key_windows(k)
        vw = key_windows(v)
        kpos = jnp.arange(Lp + 2 * BLK) - BLK
        kvalid = ((kpos >= 0) & (kpos < L)).reshape(nb + 2, BLK)
        kvalid = jnp.concatenate([kvalid[:-2], kvalid[1:-1], kvalid[2:]], axis=1)
        rel = jnp.arange(3 * BLK)[None, :] - BLK - jnp.arange(BLK)[:, None]
        mask = (jnp.abs(rel) <= half)[None] & kvalid[:, None, :]
        bias = rel_bias[t5_bucket(rel * dil)].transpose(2, 0, 1).astype(jnp.float32)
        s = jnp.einsum('gnqhd,gnkhd->gnhqk', qs, kw).astype(jnp.float32) * scale + bias[None, None]
        s = jnp.where(mask[None, :, None], s, NEG_INF)
        m = jnp.max(s, axis=-1, keepdims=True)
        e = jnp.exp(s - m)
        den = jnp.sum(e, axis=-1)
        o = jnp.einsum('gnhqk,gnkhd->gnqhd', e.astype(v.dtype), vw).astype(jnp.float32)
        o = o / jnp.swapaxes(den, 2, 3)[..., None]
        lse = jnp.swapaxes(m[..., 0] + jnp.log(den), 2, 3)
        o = o.reshape(G, Lp, H, dh)[:, :L].reshape(Bsz, dil, L, H, dh)
        outs.append(o.transpose(0, 2, 1, 3, 4).reshape(Bsz, S, H, dh))
        lse = lse.reshape(G, Lp, H)[:, :L].reshape(Bsz, dil, L, H)
        lses.append(lse.transpose(0, 2, 1, 3).reshape(Bsz, S, H))
    wts = jax.nn.softmax(jnp.stack(lses), axis=0)
    o = jnp.sum(wts[..., None] * jnp.stack(outs), axis=0)
    return o.reshape(Bsz, S, H * dh).astype(q.dtype)


def hybrid_attention(h, w_in, q_norm, kv_norm, w_uq, w_ukv, w_out, rel_bias):
    B, S, _ = h.shape
    z = h @ w_in
    cuts = []
    acc = 0
    for wdt in IN_WIDTHS[:-1]:
        acc += wdt
        cuts.append(acc)
    c_q, c_kv, k_r, q_b, k_b, v_b = jnp.split(z, cuts, axis=-1)
    a_out = mla_attention(c_q, c_kv, k_r, q_norm, kv_norm, w_uq, w_ukv)
    heads = lambda t: t.reshape(B, S, DIL_HEADS, DIL_HEAD_DIM)
    b_out = dilated_attention(heads(q_b), heads(k_b), heads(v_b), rel_bias)
    return jnp.concatenate([a_out, b_out], axis=-1) @ w_out


def wkv7_scan(r, w, k, v, a, b, reverse):
    Bsz, S, H, N = r.shape

    def step(state, inp):
        r_t, w_t, k_t, v_t, a_t, b_t = inp
        sa = jnp.einsum('bhvk,bhk->bhv', state, a_t)
        state = (state * w_t[:, :, None, :] + sa[..., None] * b_t[:, :, None, :]
                 + v_t[..., None] * k_t[:, :, None, :])
        return state, jnp.einsum('bhvk,bhk->bhv', state, r_t)

    seq_major = tuple(jnp.swapaxes(t, 0, 1) for t in (r, w, k, v, a, b))
    state0 = jnp.zeros((Bsz, H, N, N), jnp.float32)
    _, ys = lax.scan(step, state0, seq_major, reverse=reverse)
    return jnp.swapaxes(ys, 0, 1)


def rwkv7_time_mix(x, mu, w_r, w_k, w_v, w_o, w0, w1, w2, a0, a1, a2, g1, g2,
                   k_k, k_a, r_k, ln_w, ln_b):
    B, S, D = x.shape
    H, N = RWKV_HEADS, RWKV_HEAD
    zero = jnp.zeros_like(x[:, :1])
    x_prev = jnp.concatenate([zero, x[:, :-1]], axis=1)
    x_next = jnp.concatenate([x[:, 1:], zero], axis=1)
    xx = 0.5 * (x_prev + x_next) - x
    xr, xw, xk, xv, xa, xg = (x + xx * mu[i] for i in range(6))
    r = xr @ w_r
    k = xk @ w_k
    v = xv @ w_v
    g = jax.nn.sigmoid(xg @ g1) @ g2
    heads = lambda t: t.reshape(B, S, H, N).astype(jnp.float32)
    kk = heads(k * k_k)
    kk = kk * lax.rsqrt(jnp.sum(kk * kk, axis=-1, keepdims=True) + 1e-12)
    rh = heads(r)
    vh = heads(v)
    ys = []
    for d in range(2):
        w_log = -jax.nn.softplus(-(w0[d] + jnp.tanh(xw @ w1[d]) @ w2[d])) - 0.5
        decay = jnp.exp(-jnp.exp(heads(w_log)))
        a = jax.nn.sigmoid(a0[d] + (xa @ a1[d]) @ a2[d])
        kd = heads(k * (1 + (a - 1) * k_a))
        ys.append(wkv7_scan(rh, decay, kd, vh, -kk, kk * heads(a), reverse=(d == 1)))
    y = ys[0] + ys[1]
    mean = jnp.mean(y, axis=-1, keepdims=True)
    var = jnp.mean(jnp.square(y - mean), axis=-1, keepdims=True)
    y = ((y - mean) * lax.rsqrt(var + RWKV_LN_EPS)).reshape(B, S, D)
    y = y * ln_w.astype(jnp.float32) + ln_b.astype(jnp.float32)
    bonus = jnp.sum(rh * heads(k) * r_k.astype(jnp.float32), axis=-1, keepdims=True) * vh
    y = y + bonus.reshape(B, S, D)
    return (y * g.astype(jnp.float32)).astype(x.dtype) @ w_o


def swiglu(h, w_gate, w_up, w_down):
    return (jax.nn.silu(h @ w_gate) * (h @ w_up)) @ w_down


def trunk(x, p):
    for layer in range(DEPTH):
        i = layer // 2
        g = p["norm_g"][layer]
        h = rms_norm(x, g[0])
        if layer % 2 == 0:
            mix = hybrid_attention(h, p["at_w_in"][i], p["at_q_norm"][i], p["at_kv_norm"][i],
                                   p["at_w_uq"][i], p["at_w_ukv"][i], p["at_w_out"][i],
                                   p["rel_bias"])
        else:
            mix = rwkv7_time_mix(h, p["rw_mu"][i], p["rw_w_r"][i], p["rw_w_k"][i], p["rw_w_v"][i],
                                 p["rw_w_o"][i], p["rw_w0"][i], p["rw_w1"][i], p["rw_w2"][i],
                                 p["rw_a0"][i], p["rw_a1"][i], p["rw_a2"][i], p["rw_g1"][i],
                                 p["rw_g2"][i], p["rw_k_k"][i], p["rw_k_a"][i], p["rw_r_k"][i],
                                 p["rw_ln_w"][i], p["rw_ln_b"][i])
        x = x + rms_norm(mix, g[1])
        h = rms_norm(x, g[2])
        x = x + rms_norm(swiglu(h, p["ffn_w_gate"][layer], p["ffn_w_up"][layer],
                                p["ffn_w_down"][layer]), g[3])
    return x


def setup_inputs(seed: int = 0) -> dict:
    key = jax.random.key(seed)
    ks = iter(jax.random.split(key, 48))
    nrm = lambda shape, scale: scale * jax.random.normal(next(ks), shape, jnp.float32)
    D, F = D_MODEL, FFN_HIDDEN
    NA, NR = N_ATTN_LAYERS, N_RWKV_LAYERS
    H, N = RWKV_HEADS, RWKV_HEAD
    return {
        "x_prompt": nrm((BATCH, SEQ, D), 1.0),
        "x_sample": nrm((DEC_BATCH, DEC_SEQ, D), 1.0),
        "norm_g": 1.0 + nrm((DEPTH, 4, D), 0.05),
        "rel_bias": nrm((N_BUCKETS, DIL_HEADS), 0.5),
        "at_w_in": nrm((NA, D, IN_DIM), D ** -0.5),
        "at_q_norm": 1.0 + nrm((NA, MLA_Q_RANK), 0.05),
        "at_kv_norm": 1.0 + nrm((NA, MLA_KV_RANK), 0.05),
        "at_w_uq": nrm((NA, MLA_Q_RANK, MLA_HEADS * (MLA_NOPE_DIM + MLA_ROPE_DIM)), MLA_Q_RANK ** -0.5),
        "at_w_ukv": nrm((NA, MLA_KV_RANK, MLA_HEADS * (MLA_NOPE_DIM + MLA_V_DIM)), MLA_KV_RANK ** -0.5),
        "at_w_out": nrm((NA, MIX_WIDTH, D), MIX_WIDTH ** -0.5),
        "rw_mu": jax.random.uniform(next(ks), (NR, 6, D), jnp.float32),
        "rw_w_r": nrm((NR, D, D), D ** -0.5),
        "rw_w_k": nrm((NR, D, D), D ** -0.5),
        "rw_w_v": nrm((NR, D, D), D ** -0.5),
        "rw_w_o": nrm((NR, D, D), D ** -0.5),
        "rw_w0": nrm((NR, 2, D), 1.0) - 2.5,
        "rw_w1": nrm((NR, 2, D, RWKV_DECAY_LORA), D ** -0.5),
        "rw_w2": nrm((NR, 2, RWKV_DECAY_LORA, D), 0.1 * RWKV_DECAY_LORA ** -0.5),
        "rw_a0": nrm((NR, 2, D), 0.3),
        "rw_a1": nrm((NR, 2, D, RWKV_A_LORA), D ** -0.5),
        "rw_a2": nrm((NR, 2, RWKV_A_LORA, D), 0.3 * RWKV_A_LORA ** -0.5),
        "rw_g1": nrm((NR, D, RWKV_GATE_LORA), D ** -0.5),
        "rw_g2": nrm((NR, RWKV_GATE_LORA, D), RWKV_GATE_LORA ** -0.5),
        "rw_k_k": 0.85 + nrm((NR, D), 0.05),
        "rw_k_a": 1.0 + nrm((NR, D), 0.05),
        "rw_r_k": nrm((NR, H, N), 0.1),
        "rw_ln_w": 1.0 + nrm((NR, D), 0.05),
        "rw_ln_b": nrm((NR, D), 0.02),
        "ffn_w_gate": nrm((DEPTH, D, F), D ** -0.5),
        "ffn_w_up": nrm((DEPTH, D, F), D ** -0.5),
        "ffn_w_down": nrm((DEPTH, F, D), F ** -0.5),
    }


def reference(x_prompt, x_sample, norm_g, rel_bias, at_w_in, at_q_norm, at_kv_norm, at_w_uq,
              at_w_ukv, at_w_out, rw_mu, rw_w_r, rw_w_k, rw_w_v, rw_w_o, rw_w0, rw_w1, rw_w2,
              rw_a0, rw_a1, rw_a2, rw_g1, rw_g2, rw_k_k, rw_k_a, rw_r_k, rw_ln_w, rw_ln_b,
              ffn_w_gate, ffn_w_up, ffn_w_down):
    params = {
        "norm_g": norm_g, "rel_bias": rel_bias,
        "at_w_in": at_w_in, "at_q_norm": at_q_norm, "at_kv_norm": at_kv_norm,
        "at_w_uq": at_w_uq, "at_w_ukv": at_w_ukv, "at_w_out": at_w_out,
        "rw_mu": rw_mu, "rw_w_r": rw_w_r, "rw_w_k": rw_w_k, "rw_w_v": rw_w_v, "rw_w_o": rw_w_o,
        "rw_w0": rw_w0, "rw_w1": rw_w1, "rw_w2": rw_w2,
        "rw_a0": rw_a0, "rw_a1": rw_a1, "rw_a2": rw_a2,
        "rw_g1": rw_g1, "rw_g2": rw_g2, "rw_k_k": rw_k_k, "rw_k_a": rw_k_a, "rw_r_k": rw_r_k,
        "rw_ln_w": rw_ln_w, "rw_ln_b": rw_ln_b,
        "ffn_w_gate": ffn_w_gate, "ffn_w_up": ffn_w_up, "ffn_w_down": ffn_w_down,
    }
    y_prompt = trunk(x_prompt, params)
    y_sample = trunk(x_sample, params)
    return (y_prompt, y_sample)
```

```python
import functools
import math

import jax
import jax.numpy as jnp
import numpy as np
from jax import lax
from jax.experimental import pallas as pl
from jax.experimental.pallas import tpu as pltpu

F32 = jnp.float32
BF16 = jnp.bfloat16

MLA_NOPE_DIM = 128
MLA_ROPE_DIM = 64
MLA_V_DIM = 128
ROPE_BASE = 10000.0
DIL_HEAD_DIM = 128
DIL_PATTERNS = ((128, 1), (512, 4), (2048, 16))
N_BUCKETS = 32
T5_MAX_DISTANCE = 1024
RWKV_HEAD = 64
RWKV_LN_EPS = 64e-5
NORM_EPS = 1e-6
NEG_BIG = -1e30
LOG2E = 1.4426950408889634

LANES = 128
MXU_DIM = 256
VMEM_LIMIT = 56 * 1024 * 1024

SCAN_CHUNK = 64
SCAN_LANES = 512


def _cparams(sem, vmem=VMEM_LIMIT):
    return pltpu.CompilerParams(dimension_semantics=sem, vmem_limit_bytes=vmem)


def _rms(x, g):
    return x * lax.rsqrt(jnp.mean(x * x, axis=-1, keepdims=True) + NORM_EPS) * g


def _dot(a, b):
    return jnp.dot(a, b, preferred_element_type=F32)


def _dot_nt(a, b):
    return lax.dot_general(a, b, (((1,), (1,)), ((), ())), preferred_element_type=F32)


def _dot_tn(a, b):
    return lax.dot_general(a, b, (((0,), (0,)), ((), ())), preferred_element_type=F32)


def _sigmoid(x):
    return 1.0 / (1.0 + jnp.exp(-x))


def _fused_mm_kernel(*refs, n_rows, n_vecs, n_extra, n_mix, tiles_per_mix, prologue, epilogue):
    rows = refs[:n_rows]
    vecs = refs[n_rows:n_rows + n_vecs]
    w_ref = refs[n_rows + n_vecs]
    base = n_rows + n_vecs + 1
    extras = refs[base:base + n_extra]
    o_ref = refs[base + n_extra]
    h_ref = refs[base + n_extra + 1]
    j = pl.program_id(1)

    @pl.when(j == 0)
    def _():
        hs = prologue(*[r[...] for r in rows], *[v[...] for v in vecs])
        for m in range(n_mix):
            h_ref[m] = hs[m].astype(BF16)

    if n_mix == 1:
        h = h_ref[0]
    else:
        h = h_ref[j // tiles_per_mix]
    acc = _dot(h, w_ref[...])
    epilogue(j, acc, o_ref, *extras)


def fused_mm(rows, vecs, w, prologue, epilogue, *, out_dtype, tm, tn, row_cols=None,
             row_width=None, n_mix=1, tiles_per_mix=1, extras=(), name):
    T = rows[0].shape[0]
    K, N = w.shape
    row_cols = row_cols or [0] * len(rows)
    row_width = row_width or K
    assert T % tm == 0 and N % tn == 0, (T, tm, N, tn)
    in_specs = [pl.BlockSpec((tm, row_width), functools.partial(lambda i, j, c: (i, c), c=c))
                for c in row_cols]
    in_specs += [pl.BlockSpec(v.shape, lambda i, j: (0, 0)) for v in vecs]
    in_specs += [pl.BlockSpec((K, tn), lambda i, j: (0, j))]
    in_specs += [spec for _, spec in extras]
    kern = functools.partial(
        _fused_mm_kernel, n_rows=len(rows), n_vecs=len(vecs), n_extra=len(extras), n_mix=n_mix,
        tiles_per_mix=tiles_per_mix, prologue=prologue, epilogue=epilogue)
    return pl.pallas_call(
        kern,
        grid=(T // tm, N // tn),
        in_specs=in_specs,
        out_specs=pl.BlockSpec((tm, tn), lambda i, j: (i, j)),
        out_shape=jax.ShapeDtypeStruct((T, N), out_dtype),
        scratch_shapes=[pltpu.VMEM((n_mix, tm, K), BF16)],
        compiler_params=_cparams(("parallel", "arbitrary")),
        name=name,
    )(*rows, *vecs, w, *[a for a, _ in extras])


def _norm_prologue(x, g):
    return (_rms(x, g),)


def _store_epilogue(j, acc, o_ref):
    o_ref[...] = acc.astype(o_ref.dtype)


def _scale_epilogue(j, acc, o_ref, s_ref):
    o_ref[...] = (acc * s_ref[...]).astype(o_ref.dtype)


def _mm_post_kernel(a_ref, b_ref, w_ref, g_ref, x_ref, o_ref):
    lhs = jnp.concatenate([a_ref[...], b_ref[...]], axis=-1)
    y = _dot(lhs, w_ref[...])
    o_ref[...] = x_ref[...] + _rms(y, g_ref[...])


def mm_post(a, b, w, g, x, *, tm, name):
    T, Ka = a.shape
    Kb = b.shape[1]
    D = w.shape[1]
    return pl.pallas_call(
        _mm_post_kernel,
        grid=(T // tm,),
        in_specs=[pl.BlockSpec((tm, Ka), lambda i: (i, 0)),
                  pl.BlockSpec((tm, Kb), lambda i: (i, 0)),
                  pl.BlockSpec((Ka + Kb, D), lambda i: (0, 0), pipeline_mode=pl.Buffered(1)),
                  pl.BlockSpec((1, D), lambda i: (0, 0)),
                  pl.BlockSpec((tm, D), lambda i: (i, 0))],
        out_specs=pl.BlockSpec((tm, D), lambda i: (i, 0)),
        out_shape=jax.ShapeDtypeStruct((T, D), F32),
        compiler_params=_cparams(("parallel",)),
        name=name,
    )(a, b, w, g, x)


def _ffn_kernel(x_ref, gpre_ref, gpost_ref, wg_ref, wu_ref, wd_ref, o_ref, h_ref, acc_ref):
    f = pl.program_id(1)

    @pl.when(f == 0)
    def _():
        h_ref[...] = _rms(x_ref[...], gpre_ref[...]).astype(BF16)
        acc_ref[...] = jnp.zeros_like(acc_ref)

    h = h_ref[...]
    gate = _dot(h, wg_ref[...])
    up = _dot(h, wu_ref[...])
    act = (gate * _sigmoid(gate) * up).astype(BF16)
    acc_ref[...] += _dot(act, wd_ref[...])

    @pl.when(f == pl.num_programs(1) - 1)
    def _():
        o_ref[...] = x_ref[...] + _rms(acc_ref[...], gpost_ref[...])


def ffn(x, g_pre, g_post, wg, wu, wd, *, tm, tf):
    T, D = x.shape
    Fh = wg.shape[1]
    assert T % tm == 0 and Fh % tf == 0
    return pl.pallas_call(
        _ffn_kernel,
        grid=(T // tm, Fh // tf),
        in_specs=[pl.BlockSpec((tm, D), lambda i, f: (i, 0)),
                  pl.BlockSpec((1, D), lambda i, f: (0, 0)),
                  pl.BlockSpec((1, D), lambda i, f: (0, 0)),
                  pl.BlockSpec((D, tf), lambda i, f: (0, f)),
                  pl.BlockSpec((D, tf), lambda i, f: (0, f)),
                  pl.BlockSpec((tf, D), lambda i, f: (f, 0))],
        out_specs=pl.BlockSpec((tm, D), lambda i, f: (i, 0)),
        out_shape=jax.ShapeDtypeStruct((T, D), F32),
        scratch_shapes=[pltpu.VMEM((tm, D), BF16), pltpu.VMEM((tm, D), F32)],
        compiler_params=_cparams(("parallel", "arbitrary")),
        name="ffn",
    )(x, g_pre, g_post, wg, wu, wd)


def _rope_groups(x, cos_t, sin_t):
    half = MLA_ROPE_DIM // 2
    n = x.shape[-1] // LANES
    lane = lax.broadcasted_iota(jnp.int32, x.shape, 1) % LANES
    partner = jnp.where(lane < half, pltpu.roll(x, x.shape[-1] - half, 1), pltpu.roll(x, half, 1))
    if n > 1:
        cos_t = jnp.concatenate([cos_t] * n, axis=-1)
        sin_t = jnp.concatenate([sin_t] * n, axis=-1)
    return x * cos_t + partner * sin_t


def _q_epilogue(j, acc, o_ref, cos_ref, sin_ref, *, scale):
    @pl.when(j == 0)
    def _():
        o_ref[...] = (acc * scale).astype(o_ref.dtype)

    @pl.when(j == 1)
    def _():
        o_ref[...] = (_rope_groups(acc, cos_ref[...], sin_ref[...]) * scale).astype(o_ref.dtype)


def _rope_k_kernel(x_ref, cos_ref, sin_ref, o_ref):
    o_ref[...] = _rope_groups(x_ref[...], cos_ref[...], sin_ref[...]).astype(o_ref.dtype)


def rope_k(lat, col_block, cos_t, sin_t, *, tm):
    T = lat.shape[0]
    return pl.pallas_call(
        _rope_k_kernel,
        grid=(T // tm,),
        in_specs=[pl.BlockSpec((tm, LANES), lambda i: (i, col_block)),
                  pl.BlockSpec((tm, LANES), lambda i: (i, 0)),
                  pl.BlockSpec((tm, LANES), lambda i: (i, 0))],
        out_specs=pl.BlockSpec((tm, LANES), lambda i: (i, 0)),
        out_shape=jax.ShapeDtypeStruct((T, LANES), BF16),
        compiler_params=_cparams(("parallel",)),
        name="rope_k",
    )(lat, cos_t, sin_t)


def _attn_update(s, v, m_ref, l_ref, acc_ref):
    m_prev = m_ref[...]
    m_new = jnp.maximum(m_prev, jnp.max(s, axis=0, keepdims=True))
    alpha = jnp.exp2(m_prev - m_new)
    p = jnp.exp2(s - m_new)
    l_ref[...] = alpha * l_ref[...] + jnp.sum(p, axis=0, keepdims=True)
    acc_ref[...] = alpha * acc_ref[...] + _dot_tn(v, p.astype(BF16))
    m_ref[...] = m_new


def _attn_init(m_ref, l_ref, acc_ref):
    m_ref[...] = jnp.full_like(m_ref, NEG_BIG)
    l_ref[...] = jnp.zeros_like(l_ref)
    acc_ref[...] = jnp.zeros_like(acc_ref)


def _attn_finish(o_ref, l_ref, acc_ref):
    o_ref[...] = (acc_ref[...] / l_ref[...]).T.astype(o_ref.dtype)


def _mla_kernel(qn_ref, qr_ref, kn_ref, kr_ref, v_ref, o_ref, m_ref, l_ref, acc_ref, *, tk):
    S = kn_ref.shape[0]
    q = jnp.concatenate([qn_ref[...], qr_ref[...]], axis=-1)
    _attn_init(m_ref, l_ref, acc_ref)

    def body(c, carry):
        rows = pl.ds(pl.multiple_of(c * tk, tk), tk)
        k = jnp.concatenate([kn_ref[rows, :], kr_ref[rows, :]], axis=-1)
        _attn_update(_dot_nt(k, q), v_ref[rows, :], m_ref, l_ref, acc_ref)
        return carry

    lax.fori_loop(0, S // tk, body, 0)
    _attn_finish(o_ref, l_ref, acc_ref)


def mla_attention(q_all, kv, krope, B, S, H, *, tq, tk):
    T = B * S
    nq = S // tq
    dv = MLA_V_DIM
    return pl.pallas_call(
        functools.partial(_mla_kernel, tk=tk),
        grid=(B, H, nq),
        in_specs=[pl.BlockSpec((tq, LANES), lambda b, h, i: (b * nq + i, h)),
                  pl.BlockSpec((tq, LANES), lambda b, h, i: (b * nq + i, H + h)),
                  pl.BlockSpec((S, LANES), lambda b, h, i: (b, h)),
                  pl.BlockSpec((S, LANES), lambda b, h, i: (b, 0)),
                  pl.BlockSpec((S, LANES), lambda b, h, i: (b, H + h))],
        out_specs=pl.BlockSpec((tq, dv), lambda b, h, i: (b * nq + i, h)),
        out_shape=jax.ShapeDtypeStruct((T, H * dv), BF16),
        scratch_shapes=[pltpu.VMEM((1, tq), F32), pltpu.VMEM((1, tq), F32),
                        pltpu.VMEM((dv, tq), F32)],
        compiler_params=_cparams(("parallel", "parallel", "arbitrary")),
        name="mla_attention",
    )(q_all, q_all, kv, krope, kv)


def _dil_kernel(q_ref, k_ref, v_ref, bias_ref, o_ref, m_ref, l_ref, acc_ref, *, tk, nside):
    S = k_ref.shape[0]
    nk = S // tk
    i = pl.program_id(2)
    q = q_ref[...]
    _attn_init(m_ref, l_ref, acc_ref)

    def body(d, carry):
        c = i + d - nside

        @pl.when(jnp.logical_and(c >= 0, c < nk))
        def _():
            rows = pl.ds(pl.multiple_of(c * tk, tk), tk)
            s = _dot_nt(k_ref[rows, :], q) + bias_ref[d]
            _attn_update(s, v_ref[rows, :], m_ref, l_ref, acc_ref)

        return carry

    lax.fori_loop(0, 2 * nside + 1, body, 0)
    _attn_finish(o_ref, l_ref, acc_ref)


def dilated_attention(qkv, bias_t, B, S, H, *, t):
    T = B * S
    nq = S // t
    nd = bias_t.shape[0]
    dh = DIL_HEAD_DIM
    return pl.pallas_call(
        functools.partial(_dil_kernel, tk=t, nside=(nd - 1) // 2),
        grid=(B, H, nq),
        in_specs=[pl.BlockSpec((t, dh), lambda b, h, i: (b * nq + i, h)),
                  pl.BlockSpec((S, dh), lambda b, h, i: (b, H + h)),
                  pl.BlockSpec((S, dh), lambda b, h, i: (b, 2 * H + h)),
                  pl.BlockSpec((nd, None, t, t), lambda b, h, i: (0, h, 0, 0))],
        out_specs=pl.BlockSpec((t, dh), lambda b, h, i: (b * nq + i, h)),
        out_shape=jax.ShapeDtypeStruct((T, H * dh), BF16),
        scratch_shapes=[pltpu.VMEM((1, t), F32), pltpu.VMEM((1, t), F32),
                        pltpu.VMEM((dh, t), F32)],
        compiler_params=_cparams(("parallel", "parallel", "arbitrary")),
        name="dilated_attention",
    )(qkv, qkv, qkv, bias_t)


def _t5_bucket_np(rel):
    half = N_BUCKETS // 2
    max_exact = half // 2
    bucket = np.where(rel > 0, half, 0)
    n = np.abs(rel)
    nf = np.maximum(n, 1).astype(np.float64)
    large = max_exact + (np.log(nf / max_exact) / math.log(T5_MAX_DISTANCE / max_exact)
                         * (half - max_exact)).astype(np.int64)
    large = np.minimum(large, half - 1)
    return bucket + np.where(n < max_exact, n, large)


def dilated_bias_tiles(rel_bias, t):
    reach = max(w // 2 for w, _ in DIL_PATTERNS)
    nside = -(-reach // t)
    d = np.arange(-nside, nside + 1)[:, None, None] * t
    delta = d + np.arange(t)[None, :, None] - np.arange(t)[None, None, :]
    mult = np.zeros(delta.shape, np.int64)
    for window, dil in DIL_PATTERNS:
        mult += ((delta % dil == 0) & (np.abs(delta) <= window // 2)).astype(np.int64)
    bucket = _t5_bucket_np(delta)
    logm = np.log(np.maximum(mult, 1)).astype(np.float32)
    b = rel_bias.astype(F32)[bucket]
    b = (b + logm[..., None]) * LOG2E
    b = jnp.where((mult > 0)[..., None], b, NEG_BIG)
    return jnp.transpose(b, (0, 3, 1, 2))


def _shift_prologue(x, xp, xn, g, mu):
    h = _rms(x, g)
    xx = 0.5 * (_rms(xp, g) + _rms(xn, g)) - h
    return tuple(h + xx * mu[m:m + 1, :] for m in range(mu.shape[0]))


def _lora1_epilogue(j, acc, o_ref):
    @pl.when(j == 0)
    def _():
        o_ref[...] = _sigmoid(acc).astype(o_ref.dtype)

    @pl.when(j == 1)
    def _():
        o_ref[...] = jnp.tanh(acc).astype(o_ref.dtype)

    @pl.when(j == 2)
    def _():
        o_ref[...] = acc.astype(o_ref.dtype)


def _lora2_kernel(a_ref, w_ref, o_ref):
    o_ref[...] = _dot(a_ref[...], w_ref[...])


def lora2(l1, w, *, tm, tn, kb):
    T = l1.shape[0]
    N = w.shape[1]
    D = N // 5

    def a_map(i, j):
        return (i, (j * tn + D) // (2 * D))

    return pl.pallas_call(
        _lora2_kernel,
        grid=(T // tm, N // tn),
        in_specs=[pl.BlockSpec((tm, kb), a_map), pl.BlockSpec((kb, tn), lambda i, j: (0, j))],
        out_specs=pl.BlockSpec((tm, tn), lambda i, j: (i, j)),
        out_shape=jax.ShapeDtypeStruct((T, N), F32),
        compiler_params=_cparams(("parallel", "parallel")),
        name="rwkv_lora2",
    )(l1, w)


def _head_block_mask(n):
    r = lax.broadcasted_iota(jnp.int32, (n, n), 0) // RWKV_HEAD
    c = lax.broadcasted_iota(jnp.int32, (n, n), 1) // RWKV_HEAD
    return r == c


def _head_sums(x):
    ones = jnp.where(_head_block_mask(MXU_DIM), 1.0, 0.0).astype(BF16)
    parts = [_dot(x[:, g:g + MXU_DIM].astype(BF16), ones) for g in range(0, x.shape[-1], MXU_DIM)]
    return parts[0] if len(parts) == 1 else jnp.concatenate(parts, axis=-1)


def _split3(x):
    hi = x.astype(BF16)
    r1 = x - hi.astype(F32)
    mid = r1.astype(BF16)
    lo = (r1 - mid.astype(F32)).astype(BF16)
    return hi, mid, lo


def _scan_kernel(r_ref, k_ref, v_ref, wl_ref, al_ref, w0_ref, a0_ref, kk_ref, ka_ref, y_ref,
                 s_ref):
    C = SCAN_CHUNK
    HN = RWKV_HEAD
    d = pl.program_id(1)
    c = pl.program_id(3)
    fwd = d == 0

    @pl.when(c == 0)
    def _():
        s_ref[...] = jnp.zeros_like(s_ref)

    r = r_ref[...]
    k = k_ref[...]
    v = v_ref[...]
    kk = k * kk_ref[...]
    kk = kk * lax.rsqrt(_head_sums(kk * kk) + 1e-12)
    z = w0_ref[...] + wl_ref[...]
    w_log = -(jnp.maximum(-z, 0.0) + jnp.log(1.0 + jnp.exp(-jnp.abs(z)))) - 0.5
    lw = -jnp.exp(w_log)
    a = _sigmoid(a0_ref[...] + al_ref[...])
    kd = k * (1.0 + (a - 1.0) * ka_ref[...])
    aa = -kk
    bb = kk * a

    sign = 1 - 2 * d
    trow = lax.broadcasted_iota(jnp.int32, (C, C), 0)
    tcol = lax.broadcasted_iota(jnp.int32, (C, C), 1)
    cum = jnp.where((trow - tcol) * sign >= 0, 1.0, 0.0).astype(BF16)
    hi, mid, lo = _split3(lw)
    l_inc = _dot(cum, hi) + _dot(cum, mid) + _dot(cum, lo)
    l_exc = l_inc - lw
    l_ref = l_inc[C // 2:C // 2 + 1, :]
    l_end = jnp.where(fwd, l_inc[C - 1:C, :], l_inc[0:1, :])
    e_in = jnp.exp(l_inc - l_ref)
    e_ex = jnp.exp(l_exc - l_ref)
    e_ng = jnp.exp(l_ref - l_inc)
    rho = jnp.exp(l_ref)
    p_end = jnp.exp(l_end)
    end_over_ref = jnp.exp(l_end - l_ref)
    at = aa * e_ex
    rt = r * e_in
    bt = bb * e_ng
    kt = kd * e_ng
    at_s = at * rho
    rt_s = rt * rho
    bh = bt * end_over_ref
    kh = kt * end_over_ref

    lane = lax.broadcasted_iota(jnp.int32, (1, LANES), 1)
    first = lane < HN

    def mstack(x):
        zero = jnp.zeros_like(x)
        return jnp.concatenate([jnp.where(first, x, zero), jnp.where(first, zero, x)], axis=0)

    trow = lax.broadcasted_iota(jnp.int32, (C, LANES), 0)
    scol = lax.broadcasted_iota(jnp.int32, (C, LANES), 1) % HN
    strict = (trow - scol) * sign > 0
    incl = (trow - scol) * sign >= 0
    eye = jnp.where(scol == trow, 1.0, 0.0)
    bdiag = _head_block_mask(LANES)

    for p in range(r.shape[-1] // LANES):
        sl = slice(p * LANES, (p + 1) * LANES)
        v_p = v[:, sl]
        v_b = v_p.astype(BF16)
        lhs = jnp.concatenate([at[:, sl], rt[:, sl]], axis=0).astype(BF16)
        rhs = jnp.concatenate([mstack(bt[:, sl].astype(BF16)),
                               mstack(kt[:, sl].astype(BF16))], axis=0)
        sc = _dot_nt(lhs, rhs)
        a_ab = jnp.where(strict, sc[:C, :LANES], 0.0)
        a_ak = jnp.where(strict, sc[:C, LANES:], 0.0)
        a_rb = jnp.where(incl, sc[C:, :LANES], 0.0)
        a_rk = jnp.where(incl, sc[C:, LANES:], 0.0)

        xb = a_ab.astype(BF16)
        tinv = eye + a_ab
        xp = _dot(xb, mstack(xb))
        n_sq = int(math.log2(C)) - 2
        for _ in range(n_sq):
            xpb = xp.astype(BF16)
            both = _dot(jnp.concatenate([tinv.astype(BF16), xpb], axis=0), mstack(xpb))
            tinv = tinv + both[:C]
            xp = both[C:]
        tinv = tinv + _dot(tinv.astype(BF16), mstack(xp.astype(BF16)))

        av = _dot(a_ak.astype(BF16), mstack(v_b))
        uw = _dot(tinv.astype(BF16),
                  jnp.concatenate([mstack(av.astype(BF16)), mstack(at_s[:, sl].astype(BF16))],
                                  axis=1))
        u_fix = uw[:, :LANES]
        w_t = uw[:, LANES:]

        s_old = s_ref[p]
        us = _dot_nt(jnp.concatenate([w_t, rt_s[:, sl]], axis=0).astype(BF16), s_old.astype(BF16))
        u = us[:C] + u_fix
        u_b = u.astype(BF16)
        y = us[C:] + _dot(jnp.concatenate([a_rb, a_rk], axis=1).astype(BF16),
                          jnp.concatenate([mstack(u_b), mstack(v_b)], axis=0))
        y_ref[:, sl] = y

        uv_t = jnp.concatenate([u, v_p], axis=0).T
        s_new = _dot(uv_t.astype(BF16),
                     jnp.concatenate([bh[:, sl], kh[:, sl]], axis=0).astype(BF16))
        s_ref[p] = jnp.where(bdiag, s_new, 0.0) + s_old * p_end[:, sl]


def rwkv_scan(rkv, l2, w0, a0, k_k, k_a, B, S, D):
    T = B * S
    C = SCAN_CHUNK
    L = min(SCAN_LANES, D)
    nc = S // C
    ng = D // L

    def row(b, d, c):
        return b * nc + c + d * (nc - 1 - 2 * c)

    def rspec(off):
        return pl.BlockSpec((C, L), lambda b, d, g, c: (row(b, d, c), off * ng + g))

    def dspec(off):
        return pl.BlockSpec((C, L), lambda b, d, g, c: (row(b, d, c), (off + d) * ng + g))

    vdir = pl.BlockSpec((None, 1, L), lambda b, d, g, c: (d, 0, g))
    vec = pl.BlockSpec((1, L), lambda b, d, g, c: (0, g))
    return pl.pallas_call(
        _scan_kernel,
        grid=(B, 2, ng, nc),
        in_specs=[rspec(0), rspec(1), rspec(2), dspec(1), dspec(3), vdir, vdir, vec, vec],
        out_specs=pl.BlockSpec((None, C, L), lambda b, d, g, c: (d, row(b, d, c), g)),
        out_shape=jax.ShapeDtypeStruct((2, T, D), F32),
        scratch_shapes=[pltpu.VMEM((L // LANES, LANES, LANES), F32)],
        compiler_params=_cparams(("parallel", "parallel", "parallel", "arbitrary")),
        name="rwkv_scan",
    )(rkv, rkv, rkv, l2, l2, w0, a0, k_k, k_a)


def _rwkv_out_kernel(y_ref, r_ref, k_ref, v_ref, g_ref, rk_ref, lnw_ref, lnb_ref, wo_ref,
                     gpost_ref, x_ref, o_ref):
    inv_n = 1.0 / RWKV_HEAD
    y = y_ref[0] + y_ref[1]
    mean = _head_sums(y) * inv_n
    yc = y - mean
    var = _head_sums(yc * yc) * inv_n
    yn = yc * lax.rsqrt(var + RWKV_LN_EPS) * lnw_ref[...] + lnb_ref[...]
    bonus = _head_sums(r_ref[...] * k_ref[...] * rk_ref[...]) * v_ref[...]
    mix = ((yn + bonus) * g_ref[...]).astype(BF16)
    out = _dot(mix, wo_ref[...])
    o_ref[...] = x_ref[...] + _rms(out, gpost_ref[...])


def rwkv_out(y, rkv, l2, r_k, ln_w, ln_b, w_o, g_post, x, *, tm):
    T, D = x.shape
    vec = pl.BlockSpec((1, D), lambda i: (0, 0))
    return pl.pallas_call(
        _rwkv_out_kernel,
        grid=(T // tm,),
        in_specs=[pl.BlockSpec((2, tm, D), lambda i: (0, i, 0)),
                  pl.BlockSpec((tm, D), lambda i: (i, 0)),
                  pl.BlockSpec((tm, D), lambda i: (i, 1)),
                  pl.BlockSpec((tm, D), lambda i: (i, 2)),
                  pl.BlockSpec((tm, D), lambda i: (i, 0)),
                  vec, vec, vec,
                  pl.BlockSpec((D, D), lambda i: (0, 0), pipeline_mode=pl.Buffered(1)),
                  vec,
                  pl.BlockSpec((tm, D), lambda i: (i, 0))],
        out_specs=pl.BlockSpec((tm, D), lambda i: (i, 0)),
        out_shape=jax.ShapeDtypeStruct((T, D), F32),
        compiler_params=_cparams(("parallel",)),
        name="rwkv_out",
    )(y, rkv, rkv, rkv, l2, r_k, ln_w, ln_b, w_o, g_post, x)


def _pad_cols(w, n):
    return jnp.pad(w, ((0, 0), (0, n - w.shape[1])))


def _pad_rows(w, n):
    return jnp.pad(w, ((0, n - w.shape[0]), (0, 0)))


def _attn_params(w_in, q_norm, kv_norm, w_uq, w_ukv, w_out, H):
    qr, kvr = q_norm.shape[0], kv_norm.shape[0]
    rope, nope, vd = MLA_ROPE_DIM, MLA_NOPE_DIM, MLA_V_DIM
    w_lat = jnp.concatenate([w_in[:, :qr + kvr], _pad_cols(w_in[:, qr + kvr:qr + kvr + rope], LANES)],
                            axis=1).astype(BF16)
    w_dil = w_in[:, qr + kvr + rope:].astype(BF16)
    uq = w_uq.reshape(qr, H, nope + rope)
    uq_rope = jnp.pad(uq[:, :, nope:], ((0, 0), (0, 0), (0, LANES - rope)))
    w_q = jnp.concatenate([uq[:, :, :nope].reshape(qr, H * nope), uq_rope.reshape(qr, H * LANES)],
                          axis=1).astype(BF16)
    ukv = w_ukv.reshape(kvr, H, nope + vd)
    w_kv = jnp.concatenate([ukv[:, :, :nope].reshape(kvr, H * nope),
                            ukv[:, :, nope:].reshape(kvr, H * vd)], axis=1).astype(BF16)
    return dict(w_lat=w_lat, w_dil=w_dil, w_q=w_q, w_kv=w_kv, w_out=w_out.astype(BF16),
                q_norm=q_norm[None], kv_norm=kv_norm[None])


def _rope_tables(B, S):
    half = MLA_ROPE_DIM // 2
    inv = ROPE_BASE ** (-jnp.arange(0, MLA_ROPE_DIM, 2, dtype=F32) / MLA_ROPE_DIM)
    ang = jnp.arange(S, dtype=F32)[:, None] * inv[None, :]
    cos, sin = jnp.cos(ang), jnp.sin(ang)
    zeros = jnp.zeros((S, LANES - 2 * half), F32)
    cos_t = jnp.concatenate([cos, cos, zeros], axis=1)
    sin_t = jnp.concatenate([-sin, sin, zeros], axis=1)
    return jnp.tile(cos_t, (B, 1)), jnp.tile(sin_t, (B, 1))


def _rwkv_params(mu, w_r, w_k, w_v, w_o, w0, w1, w2, a0, a1, a2, g1, g2, k_k, k_a, r_k, ln_w,
                 ln_b):
    D = w_r.shape[0]
    kb = MXU_DIM
    w_rkv = jnp.concatenate([w_r, w_k, w_v], axis=1).astype(BF16)
    w_l1 = jnp.concatenate([_pad_cols(g1, kb), _pad_cols(jnp.concatenate([w1[0], w1[1]], 1), kb),
                            _pad_cols(jnp.concatenate([a1[0], a1[1]], 1), kb)], axis=1).astype(BF16)
    rw = w2.shape[1]
    ra = a2.shape[1]
    z = lambda n: jnp.zeros((n, D), F32)
    w_l2 = jnp.concatenate([
        _pad_rows(g2, kb),
        _pad_rows(w2[0], kb),
        _pad_rows(jnp.concatenate([z(rw), w2[1]], 0), kb),
        _pad_rows(a2[0], kb),
        _pad_rows(jnp.concatenate([z(ra), a2[1]], 0), kb)], axis=1).astype(BF16)
    return dict(w_rkv=w_rkv, w_l1=w_l1, w_l2=w_l2, w_o=w_o.astype(BF16),
                mu_rkv=jnp.stack([mu[0], mu[2], mu[3]]), mu_l1=jnp.stack([mu[5], mu[1], mu[4]]),
                w0=w0[:, None, :], a0=a0[:, None, :], k_k=k_k[None], k_a=k_a[None],
                r_k=r_k.reshape(1, D), ln_w=ln_w[None], ln_b=ln_b[None])


def _tile(n, pref):
    t = min(pref, n)
    while n % t:
        t //= 2
    return t


def attention_layer(x, B, S, g_pre, g_post, ap, bias_t, cos_t, sin_t, H):
    T, D = x.shape
    tm = _tile(T, 512)
    qr = ap["q_norm"].shape[1]
    n_lat = ap["w_lat"].shape[1]
    lat = fused_mm([x], [g_pre], ap["w_lat"], _norm_prologue, _store_epilogue, out_dtype=F32,
                   tm=tm, tn=n_lat, name="attn_latent")
    n_dil = ap["w_dil"].shape[1]
    dil_scale = jnp.concatenate([jnp.full((1, n_dil // 3), DIL_HEAD_DIM ** -0.5 * LOG2E, F32),
                                 jnp.ones((1, 2 * n_dil // 3), F32)], axis=1)
    qkv_b = fused_mm([x], [g_pre], ap["w_dil"], _norm_prologue, _scale_epilogue, out_dtype=BF16,
                     tm=tm, tn=n_dil // 3,
                     extras=[(dil_scale, pl.BlockSpec((1, n_dil // 3), lambda i, j: (0, j)))],
                     name="attn_dil_qkv")
    q_scale = (MLA_NOPE_DIM + MLA_ROPE_DIM) ** -0.5 * LOG2E
    nq = ap["w_q"].shape[1]
    tab = pl.BlockSpec((tm, LANES), lambda i, j: (i, 0))
    q_all = fused_mm([lat], [ap["q_norm"]], ap["w_q"], _norm_prologue,
                     functools.partial(_q_epilogue, scale=q_scale), out_dtype=BF16, tm=tm,
                     tn=nq // 2, row_cols=[0], row_width=qr, extras=[(cos_t, tab), (sin_t, tab)],
                     name="mla_q")
    kv = fused_mm([lat], [ap["kv_norm"]], ap["w_kv"], _norm_prologue, _store_epilogue,
                  out_dtype=BF16, tm=tm, tn=ap["w_kv"].shape[1] // 2, row_cols=[1], row_width=qr,
                  name="mla_kv")
    krope = rope_k(lat, (n_lat - LANES) // LANES, cos_t, sin_t, tm=tm)
    a_out = mla_attention(q_all, kv, krope, B, S, H, tq=_tile(S, 256), tk=_tile(S, 512))
    b_out = dilated_attention(qkv_b, bias_t, B, S, H, t=bias_t.shape[-1])
    return mm_post(a_out, b_out, ap["w_out"], g_post, x, tm=_tile(T, 256), name="attn_out")


def rwkv_layer(x, B, S, g_pre, g_post, rp):
    T, D = x.shape
    x3 = x.reshape(B, S, D)
    zero = jnp.zeros((B, 1, D), F32)
    xp = jnp.concatenate([zero, x3[:, :-1]], axis=1).reshape(T, D)
    xn = jnp.concatenate([x3[:, 1:], zero], axis=1).reshape(T, D)
    tm = _tile(T, 256)
    rkv = fused_mm([x, xp, xn], [g_pre, rp["mu_rkv"]], rp["w_rkv"], _shift_prologue,
                   _store_epilogue, out_dtype=F32, tm=tm, tn=D // 2, n_mix=3, tiles_per_mix=2,
                   name="rwkv_rkv")
    l1 = fused_mm([x, xp, xn], [g_pre, rp["mu_l1"]], rp["w_l1"], _shift_prologue,
                  _lora1_epilogue, out_dtype=BF16, tm=tm, tn=MXU_DIM, n_mix=3, tiles_per_mix=1,
                  name="rwkv_lora1")
    l2 = lora2(l1, rp["w_l2"], tm=_tile(T, 512), tn=D // 2 if D >= 512 else D, kb=MXU_DIM)
    y = rwkv_scan(rkv, l2, rp["w0"], rp["a0"], rp["k_k"], rp["k_a"], B, S, D)
    return rwkv_out(y, rkv, l2, rp["r_k"], rp["ln_w"], rp["ln_b"], rp["w_o"], g_post, x,
                    tm=_tile(T, 256))


def trunk(x3, p):
    B, S, D = x3.shape
    x = x3.reshape(B * S, D)
    depth = p["norm_g"].shape[0]
    for layer in range(depth):
        i = layer // 2
        g = p["norm_g"][layer]
        if layer % 2 == 0:
            x = attention_layer(x, B, S, g[0][None], g[1][None], p["attn"][i], p["bias_t"],
                                p["cos_t"][(B, S)], p["sin_t"][(B, S)], p["heads"])
        else:
            x = rwkv_layer(x, B, S, g[0][None], g[1][None], p["rwkv"][i])
        x = ffn(x, g[2][None], g[3][None], p["ffn_wg"][layer], p["ffn_wu"][layer],
                p["ffn_wd"][layer], tm=_tile(B * S, 512), tf=_tile(p["ffn_wg"][layer].shape[1], 512))
    return x.reshape(B, S, D)


def kernel(x_prompt, x_sample, norm_g, rel_bias, at_w_in, at_q_norm, at_kv_norm, at_w_uq, at_w_ukv, at_w_out, rw_mu, rw_w_r, rw_w_k, rw_w_v, rw_w_o, rw_w0, rw_w1, rw_w2, rw_a0, rw_a1, rw_a2, rw_g1, rw_g2, rw_k_k, rw_k_a, rw_r_k, rw_ln_w, rw_ln_b, ffn_w_gate, ffn_w_up, ffn_w_down):
    H = rel_bias.shape[1]
    p = {"norm_g": norm_g, "heads": H}
    p["attn"] = [_attn_params(at_w_in[i], at_q_norm[i], at_kv_norm[i], at_w_uq[i], at_w_ukv[i],
                              at_w_out[i], H) for i in range(at_w_in.shape[0])]
    p["rwkv"] = [_rwkv_params(rw_mu[i], rw_w_r[i], rw_w_k[i], rw_w_v[i], rw_w_o[i], rw_w0[i],
                              rw_w1[i], rw_w2[i], rw_a0[i], rw_a1[i], rw_a2[i], rw_g1[i],
                              rw_g2[i], rw_k_k[i], rw_k_a[i], rw_r_k[i], rw_ln_w[i], rw_ln_b[i])
                 for i in range(rw_mu.shape[0])]
    p["ffn_wg"] = ffn_w_gate.astype(BF16)
    p["ffn_wu"] = ffn_w_up.astype(BF16)
    p["ffn_wd"] = ffn_w_down.astype(BF16)
    p["bias_t"] = dilated_bias_tiles(rel_bias, MXU_DIM)
    p["cos_t"], p["sin_t"] = {}, {}
    for xs in (x_prompt, x_sample):
        B, S = xs.shape[:2]
        p["cos_t"][(B, S)], p["sin_t"][(B, S)] = _rope_tables(B, S)
    return trunk(x_prompt, p), trunk(x_sample, p)
```

```python
import functools
import math

import jax
import jax.numpy as jnp
import numpy as np
from jax import lax
from jax.experimental import pallas as pl
from jax.experimental.pallas import tpu as pltpu

F32 = jnp.float32
BF16 = jnp.bfloat16

MLA_NOPE_DIM = 128
MLA_ROPE_DIM = 64
MLA_V_DIM = 128
ROPE_BASE = 10000.0
DIL_HEAD_DIM = 128
DIL_PATTERNS = ((128, 1), (512, 4), (2048, 16))
N_BUCKETS = 32
T5_MAX_DISTANCE = 1024
RWKV_HEAD = 64
RWKV_LN_EPS = 64e-5
NORM_EPS = 1e-6
NEG_BIG = -1e30
LOG2E = 1.4426950408889634

LANES = 128
MXU_DIM = 256
VMEM_LIMIT = 56 * 1024 * 1024

SCAN_CHUNK = 64
SCAN_LANES = 1024


def _cparams(sem, vmem=VMEM_LIMIT):
    return pltpu.CompilerParams(dimension_semantics=sem, vmem_limit_bytes=vmem)


def _rms(x, g):
    return x * lax.rsqrt(jnp.mean(x * x, axis=-1, keepdims=True) + NORM_EPS) * g


def _dot(a, b):
    return jnp.dot(a, b, preferred_element_type=F32)


def _dot_nt(a, b):
    return lax.dot_general(a, b, (((1,), (1,)), ((), ())), preferred_element_type=F32)


def _dot_tn(a, b):
    return lax.dot_general(a, b, (((0,), (0,)), ((), ())), preferred_element_type=F32)


def _sigmoid(x):
    return 1.0 / (1.0 + jnp.exp(-x))


def _fused_mm_kernel(*refs, n_rows, n_vecs, n_extra, n_mix, tiles_per_mix, prologue, epilogue):
    rows = refs[:n_rows]
    vecs = refs[n_rows:n_rows + n_vecs]
    w_ref = refs[n_rows + n_vecs]
    base = n_rows + n_vecs + 1
    extras = refs[base:base + n_extra]
    o_ref = refs[base + n_extra]
    h_ref = refs[base + n_extra + 1]
    j = pl.program_id(1)

    @pl.when(j == 0)
    def _():
        hs = prologue(*[r[...] for r in rows], *[v[...] for v in vecs])
        for m in range(n_mix):
            h_ref[m] = hs[m].astype(BF16)

    if n_mix == 1:
        h = h_ref[0]
    else:
        h = h_ref[j // tiles_per_mix]
    acc = _dot(h, w_ref[...])
    epilogue(j, acc, o_ref, *extras)


def fused_mm(rows, vecs, w, prologue, epilogue, *, out_dtype, tm, tn, row_cols=None,
             row_width=None, n_mix=1, tiles_per_mix=1, extras=(), name):
    T = rows[0].shape[0]
    K, N = w.shape
    row_cols = row_cols or [0] * len(rows)
    row_width = row_width or K
    assert T % tm == 0 and N % tn == 0, (T, tm, N, tn)
    in_specs = [pl.BlockSpec((tm, row_width), functools.partial(lambda i, j, c: (i, c), c=c))
                for c in row_cols]
    in_specs += [pl.BlockSpec(v.shape, lambda i, j: (0, 0)) for v in vecs]
    in_specs += [pl.BlockSpec((K, tn), lambda i, j: (0, j))]
    in_specs += [spec for _, spec in extras]
    kern = functools.partial(
        _fused_mm_kernel, n_rows=len(rows), n_vecs=len(vecs), n_extra=len(extras), n_mix=n_mix,
        tiles_per_mix=tiles_per_mix, prologue=prologue, epilogue=epilogue)
    return pl.pallas_call(
        kern,
        grid=(T // tm, N // tn),
        in_specs=in_specs,
        out_specs=pl.BlockSpec((tm, tn), lambda i, j: (i, j)),
        out_shape=jax.ShapeDtypeStruct((T, N), out_dtype),
        scratch_shapes=[pltpu.VMEM((n_mix, tm, K), BF16)],
        compiler_params=_cparams(("parallel", "arbitrary")),
        name=name,
    )(*rows, *vecs, w, *[a for a, _ in extras])


def _norm_prologue(x, g):
    return (_rms(x, g),)


def _store_epilogue(j, acc, o_ref):
    o_ref[...] = acc.astype(o_ref.dtype)


def _scale_epilogue(j, acc, o_ref, s_ref):
    o_ref[...] = (acc * s_ref[...]).astype(o_ref.dtype)


def _mm_post_kernel(a_ref, b_ref, w_ref, g_ref, x_ref, o_ref):
    lhs = jnp.concatenate([a_ref[...], b_ref[...]], axis=-1)
    y = _dot(lhs, w_ref[...])
    o_ref[...] = x_ref[...] + _rms(y, g_ref[...])


def mm_post(a, b, w, g, x, *, tm, name):
    T, Ka = a.shape
    Kb = b.shape[1]
    D = w.shape[1]
    return pl.pallas_call(
        _mm_post_kernel,
        grid=(T // tm,),
        in_specs=[pl.BlockSpec((tm, Ka), lambda i: (i, 0)),
                  pl.BlockSpec((tm, Kb), lambda i: (i, 0)),
                  pl.BlockSpec((Ka + Kb, D), lambda i: (0, 0), pipeline_mode=pl.Buffered(1)),
                  pl.BlockSpec((1, D), lambda i: (0, 0)),
                  pl.BlockSpec((tm, D), lambda i: (i, 0))],
        out_specs=pl.BlockSpec((tm, D), lambda i: (i, 0)),
        out_shape=jax.ShapeDtypeStruct((T, D), F32),
        compiler_params=_cparams(("parallel",)),
        name=name,
    )(a, b, w, g, x)


def _ffn_kernel(x_ref, gpre_ref, gpost_ref, wg_ref, wu_ref, wd_ref, o_ref, h_ref, acc_ref):
    f = pl.program_id(1)

    @pl.when(f == 0)
    def _():
        h_ref[...] = _rms(x_ref[...], gpre_ref[...]).astype(BF16)
        acc_ref[...] = jnp.zeros_like(acc_ref)

    h = h_ref[...]
    gate = _dot(h, wg_ref[...])
    up = _dot(h, wu_ref[...])
    act = (gate * _sigmoid(gate) * up).astype(BF16)
    acc_ref[...] += _dot(act, wd_ref[...])

    @pl.when(f == pl.num_programs(1) - 1)
    def _():
        o_ref[...] = x_ref[...] + _rms(acc_ref[...], gpost_ref[...])


def ffn(x, g_pre, g_post, wg, wu, wd, *, tm, tf):
    T, D = x.shape
    Fh = wg.shape[1]
    assert T % tm == 0 and Fh % tf == 0
    return pl.pallas_call(
        _ffn_kernel,
        grid=(T // tm, Fh // tf),
        in_specs=[pl.BlockSpec((tm, D), lambda i, f: (i, 0)),
                  pl.BlockSpec((1, D), lambda i, f: (0, 0)),
                  pl.BlockSpec((1, D), lambda i, f: (0, 0)),
                  pl.BlockSpec((D, tf), lambda i, f: (0, f)),
                  pl.BlockSpec((D, tf), lambda i, f: (0, f)),
                  pl.BlockSpec((tf, D), lambda i, f: (f, 0))],
        out_specs=pl.BlockSpec((tm, D), lambda i, f: (i, 0)),
        out_shape=jax.ShapeDtypeStruct((T, D), F32),
        scratch_shapes=[pltpu.VMEM((tm, D), BF16), pltpu.VMEM((tm, D), F32)],
        compiler_params=_cparams(("parallel", "arbitrary")),
        name="ffn",
    )(x, g_pre, g_post, wg, wu, wd)


def _rope_groups(x, cos_t, sin_t):
    half = MLA_ROPE_DIM // 2
    n = x.shape[-1] // LANES
    lane = lax.broadcasted_iota(jnp.int32, x.shape, 1) % LANES
    partner = jnp.where(lane < half, pltpu.roll(x, x.shape[-1] - half, 1), pltpu.roll(x, half, 1))
    if n > 1:
        cos_t = jnp.concatenate([cos_t] * n, axis=-1)
        sin_t = jnp.concatenate([sin_t] * n, axis=-1)
    return x * cos_t + partner * sin_t


def _q_epilogue(j, acc, o_ref, cos_ref, sin_ref, *, scale):
    @pl.when(j == 0)
    def _():
        o_ref[...] = (acc * scale).astype(o_ref.dtype)

    @pl.when(j == 1)
    def _():
        o_ref[...] = (_rope_groups(acc, cos_ref[...], sin_ref[...]) * scale).astype(o_ref.dtype)


def _rope_k_kernel(x_ref, cos_ref, sin_ref, o_ref):
    o_ref[...] = _rope_groups(x_ref[...], cos_ref[...], sin_ref[...]).astype(o_ref.dtype)


def rope_k(lat, col_block, cos_t, sin_t, *, tm):
    T = lat.shape[0]
    return pl.pallas_call(
        _rope_k_kernel,
        grid=(T // tm,),
        in_specs=[pl.BlockSpec((tm, LANES), lambda i: (i, col_block)),
                  pl.BlockSpec((tm, LANES), lambda i: (i, 0)),
                  pl.BlockSpec((tm, LANES), lambda i: (i, 0))],
        out_specs=pl.BlockSpec((tm, LANES), lambda i: (i, 0)),
        out_shape=jax.ShapeDtypeStruct((T, LANES), BF16),
        compiler_params=_cparams(("parallel",)),
        name="rope_k",
    )(lat, cos_t, sin_t)


def _softmax_weights(ss, m_prev):
    m_new, ps = [], []
    for h, tiles in enumerate(ss):
        m = m_prev[h]
        for s in tiles:
            m = jnp.maximum(m, jnp.max(s, axis=0, keepdims=True))
        m_new.append(m)
    for h, tiles in enumerate(ss):
        ps.append([jnp.exp2(s - m_new[h]) for s in tiles])
    return m_new, ps


def _weighted_values(ps, vs, l, acc):
    for p, v in zip(ps, vs):
        l = l + jnp.sum(p, axis=0, keepdims=True)
        acc = acc + _dot_tn(v, p.astype(BF16))
    return l, acc


def _mla_kernel(qn_ref, qr_ref, kn_ref, kr_ref, v_ref, o_ref, m_ref, l_ref, acc_ref, *, tk):
    S = kn_ref.shape[0]
    n, _, tq = acc_ref.shape
    q = jnp.concatenate([qn_ref[...], qr_ref[...]], axis=-1)
    qs = [q[h * tq:(h + 1) * tq] for h in range(n)]
    m_ref[...] = jnp.full_like(m_ref, NEG_BIG)
    l_ref[...] = jnp.zeros_like(l_ref)
    acc_ref[...] = jnp.zeros_like(acc_ref)

    def body(c, carry):
        rows = pl.ds(pl.multiple_of(c * tk, tk), tk)
        k = jnp.concatenate([kn_ref[rows, :], kr_ref[rows, :]], axis=-1)
        v = v_ref[rows, :]
        ss = [[_dot_nt(k, qh)] for qh in qs]
        m_prev = [m_ref[h] for h in range(n)]
        m_new, ps = _softmax_weights(ss, m_prev)
        for h in range(n):
            alpha = jnp.exp2(m_prev[h] - m_new[h])
            l, acc = _weighted_values(ps[h], [v], alpha * l_ref[h], alpha * acc_ref[h])
            l_ref[h] = l
            acc_ref[h] = acc
            m_ref[h] = m_new[h]
        return carry

    lax.fori_loop(0, S // tk, body, 0)
    for h in range(n):
        o_ref[h * tq:(h + 1) * tq, :] = (acc_ref[h] / l_ref[h]).T.astype(o_ref.dtype)


def mla_attention(q_all, kv, krope, B, S, H, *, tq, n_chain, tk):
    T = B * S
    tb = tq * n_chain
    nq = S // tb
    dv = MLA_V_DIM
    return pl.pallas_call(
        functools.partial(_mla_kernel, tk=tk),
        grid=(B, H, nq),
        in_specs=[pl.BlockSpec((tb, LANES), lambda b, h, i: (b * nq + i, h)),
                  pl.BlockSpec((tb, LANES), lambda b, h, i: (b * nq + i, H + h)),
                  pl.BlockSpec((S, LANES), lambda b, h, i: (b, h)),
                  pl.BlockSpec((S, LANES), lambda b, h, i: (b, 0)),
                  pl.BlockSpec((S, LANES), lambda b, h, i: (b, H + h))],
        out_specs=pl.BlockSpec((tb, dv), lambda b, h, i: (b * nq + i, h)),
        out_shape=jax.ShapeDtypeStruct((T, H * dv), BF16),
        scratch_shapes=[pltpu.VMEM((n_chain, 1, tq), F32), pltpu.VMEM((n_chain, 1, tq), F32),
                        pltpu.VMEM((n_chain, dv, tq), F32)],
        compiler_params=_cparams(("parallel", "parallel", "arbitrary")),
        name="mla_attention",
    )(q_all, q_all, kv, krope, kv)


def _dil_kernel(q_ref, k_ref, v_ref, bias_ref, o_ref, *, t, nside):
    nk = k_ref.shape[0] // t
    i = pl.program_id(2)
    q = q_ref[...]
    ss, vs = [], []
    for d in range(2 * nside + 1):
        c = i + d - nside
        inside = jnp.logical_and(c >= 0, c < nk)
        rows = pl.ds(pl.multiple_of(jnp.clip(c, 0, nk - 1) * t, t), t)
        ss.append(_dot_nt(k_ref[rows, :], q) + (bias_ref[d] + jnp.where(inside, 0.0, NEG_BIG)))
        vs.append(v_ref[rows, :])
    m_new, ps = _softmax_weights([ss], [jnp.full((1, t), NEG_BIG, F32)])
    l, acc = _weighted_values(ps[0], vs, jnp.zeros((1, t), F32), jnp.zeros((DIL_HEAD_DIM, t), F32))
    o_ref[...] = (acc / l).T.astype(o_ref.dtype)


def dilated_attention(qkv, bias_t, B, S, H, *, t):
    T = B * S
    nq = S // t
    nd = bias_t.shape[0]
    dh = DIL_HEAD_DIM
    return pl.pallas_call(
        functools.partial(_dil_kernel, t=t, nside=(nd - 1) // 2),
        grid=(B, H, nq),
        in_specs=[pl.BlockSpec((t, dh), lambda b, h, i: (b * nq + i, h)),
                  pl.BlockSpec((S, dh), lambda b, h, i: (b, H + h)),
                  pl.BlockSpec((S, dh), lambda b, h, i: (b, 2 * H + h)),
                  pl.BlockSpec((nd, None, t, t), lambda b, h, i: (0, h, 0, 0))],
        out_specs=pl.BlockSpec((t, dh), lambda b, h, i: (b * nq + i, h)),
        out_shape=jax.ShapeDtypeStruct((T, H * dh), BF16),
        compiler_params=_cparams(("parallel", "parallel", "arbitrary")),
        name="dilated_attention",
    )(qkv, qkv, qkv, bias_t)


def _t5_bucket_np(rel):
    half = N_BUCKETS // 2
    max_exact = half // 2
    bucket = np.where(rel > 0, half, 0)
    n = np.abs(rel)
    nf = np.maximum(n, 1).astype(np.float64)
    large = max_exact + (np.log(nf / max_exact) / math.log(T5_MAX_DISTANCE / max_exact)
                         * (half - max_exact)).astype(np.int64)
    large = np.minimum(large, half - 1)
    return bucket + np.where(n < max_exact, n, large)


def dilated_bias_tiles(rel_bias, t):
    reach = max(w // 2 for w, _ in DIL_PATTERNS)
    nside = -(-reach // t)
    nd = 2 * nside + 1
    span = (nside + 1) * t
    delta = np.arange(-span + 1, span)
    mult = np.zeros(delta.shape, np.int64)
    for window, dil in DIL_PATTERNS:
        mult += ((delta % dil == 0) & (np.abs(delta) <= window // 2)).astype(np.int64)
    logm = np.log(np.maximum(mult, 1)).astype(np.float32)
    f = (rel_bias.astype(F32)[_t5_bucket_np(delta)] + logm[:, None]) * LOG2E
    f = jnp.where((mult > 0)[:, None], f, NEG_BIG)
    frev = f[::-1].T
    starts = span - 1 - (np.arange(nd)[:, None] - nside) * t - np.arange(t)[None, :]
    rows = jax.vmap(lambda s: lax.dynamic_slice_in_dim(frev, s, t, axis=1))(
        jnp.asarray(starts.reshape(-1), jnp.int32))
    return jnp.transpose(rows.reshape(nd, t, frev.shape[0], t), (0, 2, 1, 3))


def _shift_prologue(x, xp, xn, g, mu):
    h = _rms(x, g)
    xx = 0.5 * (_rms(xp, g) + _rms(xn, g)) - h
    return tuple(h + xx * mu[m:m + 1, :] for m in range(mu.shape[0]))


def _lora1_epilogue(j, acc, o_ref):
    @pl.when(j == 0)
    def _():
        o_ref[...] = _sigmoid(acc).astype(o_ref.dtype)

    @pl.when(j == 1)
    def _():
        o_ref[...] = jnp.tanh(acc).astype(o_ref.dtype)

    @pl.when(j == 2)
    def _():
        o_ref[...] = acc.astype(o_ref.dtype)


def _lora2_kernel(a_ref, w_ref, o_ref):
    o_ref[...] = _dot(a_ref[...], w_ref[...])


def lora2(l1, w, *, tm, tn, kb):
    T = l1.shape[0]
    N = w.shape[1]
    D = N // 5

    def a_map(i, j):
        return (i, (j * tn + D) // (2 * D))

    return pl.pallas_call(
        _lora2_kernel,
        grid=(T // tm, N // tn),
        in_specs=[pl.BlockSpec((tm, kb), a_map), pl.BlockSpec((kb, tn), lambda i, j: (0, j))],
        out_specs=pl.BlockSpec((tm, tn), lambda i, j: (i, j)),
        out_shape=jax.ShapeDtypeStruct((T, N), F32),
        compiler_params=_cparams(("parallel", "parallel")),
        name="rwkv_lora2",
    )(l1, w)


def _head_block_mask(n):
    r = lax.broadcasted_iota(jnp.int32, (n, n), 0) // RWKV_HEAD
    c = lax.broadcasted_iota(jnp.int32, (n, n), 1) // RWKV_HEAD
    return r == c


def _head_sums(x):
    ones = jnp.where(_head_block_mask(MXU_DIM), 1.0, 0.0).astype(BF16)
    parts = [_dot(x[:, g:g + MXU_DIM].astype(BF16), ones) for g in range(0, x.shape[-1], MXU_DIM)]
    return parts[0] if len(parts) == 1 else jnp.concatenate(parts, axis=-1)


def _split3(x):
    hi = x.astype(BF16)
    r1 = x - hi.astype(F32)
    mid = r1.astype(BF16)
    lo = (r1 - mid.astype(F32)).astype(BF16)
    return hi, mid, lo


def _scan_kernel(r_ref, k_ref, v_ref, wl_ref, al_ref, w0_ref, a0_ref, kk_ref, ka_ref, y_ref,
                 s_ref):
    C = SCAN_CHUNK
    HN = RWKV_HEAD
    d = pl.program_id(1)
    c = pl.program_id(3)
    fwd = d == 0

    @pl.when(c == 0)
    def _():
        s_ref[...] = jnp.zeros_like(s_ref)

    r = r_ref[...]
    k = k_ref[...]
    v = v_ref[...]
    kk = k * kk_ref[...]
    kk = kk * lax.rsqrt(_head_sums(kk * kk) + 1e-12)
    z = w0_ref[...] + wl_ref[...]
    w_log = -(jnp.maximum(-z, 0.0) + jnp.log(1.0 + jnp.exp(-jnp.abs(z)))) - 0.5
    lw = -jnp.exp(w_log)
    a = _sigmoid(a0_ref[...] + al_ref[...])
    kd = k * (1.0 + (a - 1.0) * ka_ref[...])
    aa = -kk
    bb = kk * a

    sign = 1 - 2 * d
    trow = lax.broadcasted_iota(jnp.int32, (C, C), 0)
    tcol = lax.broadcasted_iota(jnp.int32, (C, C), 1)
    cum = jnp.where((trow - tcol) * sign >= 0, 1.0, 0.0).astype(BF16)
    hi, mid, lo = _split3(lw)
    l_inc = _dot(cum, hi) + _dot(cum, mid) + _dot(cum, lo)
    l_exc = l_inc - lw
    l_ref = l_inc[C // 2:C // 2 + 1, :]
    l_end = jnp.where(fwd, l_inc[C - 1:C, :], l_inc[0:1, :])
    e_in = jnp.exp(l_inc - l_ref)
    e_ex = jnp.exp(l_exc - l_ref)
    e_ng = jnp.exp(l_ref - l_inc)
    rho = jnp.exp(l_ref)
    p_end = jnp.exp(l_end)
    end_over_ref = jnp.exp(l_end - l_ref)
    at = aa * e_ex
    rt = r * e_in
    bt = bb * e_ng
    kt = kd * e_ng
    at_s = at * rho
    rt_s = rt * rho
    bh = bt * end_over_ref
    kh = kt * end_over_ref

    lane = lax.broadcasted_iota(jnp.int32, (1, LANES), 1)
    first = lane < HN

    def mstack(x):
        zero = jnp.zeros_like(x)
        return jnp.concatenate([jnp.where(first, x, zero), jnp.where(first, zero, x)], axis=0)

    trow = lax.broadcasted_iota(jnp.int32, (C, LANES), 0)
    scol = lax.broadcasted_iota(jnp.int32, (C, LANES), 1) % HN
    strict = (trow - scol) * sign > 0
    incl = (trow - scol) * sign >= 0
    eye = jnp.where(scol == trow, 1.0, 0.0)
    bdiag = _head_block_mask(LANES)

    groups = [slice(p * LANES, (p + 1) * LANES) for p in range(r.shape[-1] // LANES)]
    cat = jnp.concatenate
    v_b = [v[:, sl].astype(BF16) for sl in groups]
    sc = [_dot_nt(cat([at[:, sl], rt[:, sl]], axis=0).astype(BF16),
                  cat([mstack(bt[:, sl].astype(BF16)), mstack(kt[:, sl].astype(BF16))], axis=0))
          for sl in groups]
    a_ab = [jnp.where(strict, s[:C, :LANES], 0.0) for s in sc]
    a_ak = [jnp.where(strict, s[:C, LANES:], 0.0) for s in sc]
    a_r = [cat([jnp.where(incl, s[C:, :LANES], 0.0), jnp.where(incl, s[C:, LANES:], 0.0)],
               axis=1).astype(BF16) for s in sc]

    xb = [x.astype(BF16) for x in a_ab]
    tinv = [eye + x for x in a_ab]
    xp = [_dot(x, mstack(x)) for x in xb]
    av = [_dot(x.astype(BF16), mstack(vb)) for x, vb in zip(a_ak, v_b)]
    for _ in range(int(math.log2(C)) - 2):
        xb = [x.astype(BF16) for x in xp]
        both = [_dot(cat([t.astype(BF16), x], axis=0), mstack(x)) for t, x in zip(tinv, xb)]
        tinv = [t + b[:C] for t, b in zip(tinv, both)]
        xp = [b[C:] for b in both]
    tinv = [t + _dot(t.astype(BF16), mstack(x.astype(BF16))) for t, x in zip(tinv, xp)]

    uw = [_dot(t.astype(BF16),
               cat([mstack(x.astype(BF16)), mstack(at_s[:, sl].astype(BF16))], axis=1))
          for t, x, sl in zip(tinv, av, groups)]

    s_old = [s_ref[p] for p in range(len(groups))]
    us = [_dot_nt(cat([w[:, LANES:], rt_s[:, sl]], axis=0).astype(BF16), s.astype(BF16))
          for w, sl, s in zip(uw, groups, s_old)]
    u = [x[:C] + w[:, :LANES] for x, w in zip(us, uw)]
    y = [x[C:] + _dot(ar, cat([mstack(uu.astype(BF16)), mstack(vb)], axis=0))
         for x, ar, uu, vb in zip(us, a_r, u, v_b)]
    for sl, yy in zip(groups, y):
        y_ref[:, sl] = yy

    s_new = [_dot(cat([uu, v[:, sl]], axis=0).T.astype(BF16),
                  cat([bh[:, sl], kh[:, sl]], axis=0).astype(BF16))
             for uu, sl in zip(u, groups)]
    for p, sl in enumerate(groups):
        s_ref[p] = jnp.where(bdiag, s_new[p], 0.0) + s_old[p] * p_end[:, sl]


def rwkv_scan(rkv, l2, w0, a0, k_k, k_a, B, S, D):
    T = B * S
    C = SCAN_CHUNK
    L = min(SCAN_LANES, D)
    nc = S // C
    ng = D // L

    def row(b, d, c):
        return b * nc + c + d * (nc - 1 - 2 * c)

    def rspec(off):
        return pl.BlockSpec((C, L), lambda b, d, g, c: (row(b, d, c), off * ng + g))

    def dspec(off):
        return pl.BlockSpec((C, L), lambda b, d, g, c: (row(b, d, c), (off + d) * ng + g))

    vdir = pl.BlockSpec((None, 1, L), lambda b, d, g, c: (d, 0, g))
    vec = pl.BlockSpec((1, L), lambda b, d, g, c: (0, g))
    return pl.pallas_call(
        _scan_kernel,
        grid=(B, 2, ng, nc),
        in_specs=[rspec(0), rspec(1), rspec(2), dspec(1), dspec(3), vdir, vdir, vec, vec],
        out_specs=pl.BlockSpec((None, C, L), lambda b, d, g, c: (d, row(b, d, c), g)),
        out_shape=jax.ShapeDtypeStruct((2, T, D), F32),
        scratch_shapes=[pltpu.VMEM((L // LANES, LANES, LANES), F32)],
        compiler_params=_cparams(("parallel", "parallel", "parallel", "arbitrary")),
        name="rwkv_scan",
    )(rkv, rkv, rkv, l2, l2, w0, a0, k_k, k_a)


def _rwkv_out_kernel(y_ref, r_ref, k_ref, v_ref, g_ref, rk_ref, lnw_ref, lnb_ref, wo_ref,
                     gpost_ref, x_ref, o_ref):
    inv_n = 1.0 / RWKV_HEAD
    y = y_ref[0] + y_ref[1]
    mean = _head_sums(y) * inv_n
    yc = y - mean
    var = _head_sums(yc * yc) * inv_n
    yn = yc * lax.rsqrt(var + RWKV_LN_EPS) * lnw_ref[...] + lnb_ref[...]
    bonus = _head_sums(r_ref[...] * k_ref[...] * rk_ref[...]) * v_ref[...]
    mix = ((yn + bonus) * g_ref[...]).astype(BF16)
    out = _dot(mix, wo_ref[...])
    o_ref[...] = x_ref[...] + _rms(out, gpost_ref[...])


def rwkv_out(y, rkv, l2, r_k, ln_w, ln_b, w_o, g_post, x, *, tm):
    T, D = x.shape
    vec = pl.BlockSpec((1, D), lambda i: (0, 0))
    return pl.pallas_call(
        _rwkv_out_kernel,
        grid=(T // tm,),
        in_specs=[pl.BlockSpec((2, tm, D), lambda i: (0, i, 0)),
                  pl.BlockSpec((tm, D), lambda i: (i, 0)),
                  pl.BlockSpec((tm, D), lambda i: (i, 1)),
                  pl.BlockSpec((tm, D), lambda i: (i, 2)),
                  pl.BlockSpec((tm, D), lambda i: (i, 0)),
                  vec, vec, vec,
                  pl.BlockSpec((D, D), lambda i: (0, 0), pipeline_mode=pl.Buffered(1)),
                  vec,
                  pl.BlockSpec((tm, D), lambda i: (i, 0))],
        out_specs=pl.BlockSpec((tm, D), lambda i: (i, 0)),
        out_shape=jax.ShapeDtypeStruct((T, D), F32),
        compiler_params=_cparams(("parallel",)),
        name="rwkv_out",
    )(y, rkv, rkv, rkv, l2, r_k, ln_w, ln_b, w_o, g_post, x)


def _pad_cols(w, n):
    return jnp.pad(w, ((0, 0), (0, n - w.shape[1])))


def _pad_rows(w, n):
    return jnp.pad(w, ((0, n - w.shape[0]), (0, 0)))


def _attn_params(w_in, q_norm, kv_norm, w_uq, w_ukv, w_out, H):
    qr, kvr = q_norm.shape[0], kv_norm.shape[0]
    rope, nope, vd = MLA_ROPE_DIM, MLA_NOPE_DIM, MLA_V_DIM
    w_lat = jnp.concatenate([w_in[:, :qr + kvr], _pad_cols(w_in[:, qr + kvr:qr + kvr + rope], LANES)],
                            axis=1).astype(BF16)
    w_dil = w_in[:, qr + kvr + rope:].astype(BF16)
    uq = w_uq.reshape(qr, H, nope + rope)
    uq_rope = jnp.pad(uq[:, :, nope:], ((0, 0), (0, 0), (0, LANES - rope)))
    w_q = jnp.concatenate([uq[:, :, :nope].reshape(qr, H * nope), uq_rope.reshape(qr, H * LANES)],
                          axis=1).astype(BF16)
    ukv = w_ukv.reshape(kvr, H, nope + vd)
    w_kv = jnp.concatenate([ukv[:, :, :nope].reshape(kvr, H * nope),
                            ukv[:, :, nope:].reshape(kvr, H * vd)], axis=1).astype(BF16)
    return dict(w_lat=w_lat, w_dil=w_dil, w_q=w_q, w_kv=w_kv, w_out=w_out.astype(BF16),
                q_norm=q_norm[None], kv_norm=kv_norm[None])


def _rope_tables(B, S):
    half = MLA_ROPE_DIM // 2
    inv = ROPE_BASE ** (-jnp.arange(0, MLA_ROPE_DIM, 2, dtype=F32) / MLA_ROPE_DIM)
    ang = jnp.arange(S, dtype=F32)[:, None] * inv[None, :]
    cos, sin = jnp.cos(ang), jnp.sin(ang)
    zeros = jnp.zeros((S, LANES - 2 * half), F32)
    cos_t = jnp.concatenate([cos, cos, zeros], axis=1)
    sin_t = jnp.concatenate([-sin, sin, zeros], axis=1)
    return jnp.tile(cos_t, (B, 1)), jnp.tile(sin_t, (B, 1))


def _rwkv_params(mu, w_r, w_k, w_v, w_o, w0, w1, w2, a0, a1, a2, g1, g2, k_k, k_a, r_k, ln_w,
                 ln_b):
    D = w_r.shape[0]
    kb = MXU_DIM
    w_rkv = jnp.concatenate([w_r, w_k, w_v], axis=1).astype(BF16)
    w_l1 = jnp.concatenate([_pad_cols(g1, kb), _pad_cols(jnp.concatenate([w1[0], w1[1]], 1), kb),
                            _pad_cols(jnp.concatenate([a1[0], a1[1]], 1), kb)], axis=1).astype(BF16)
    rw = w2.shape[1]
    ra = a2.shape[1]
    z = lambda n: jnp.zeros((n, D), F32)
    w_l2 = jnp.concatenate([
        _pad_rows(g2, kb),
        _pad_rows(w2[0], kb),
        _pad_rows(jnp.concatenate([z(rw), w2[1]], 0), kb),
        _pad_rows(a2[0], kb),
        _pad_rows(jnp.concatenate([z(ra), a2[1]], 0), kb)], axis=1).astype(BF16)
    return dict(w_rkv=w_rkv, w_l1=w_l1, w_l2=w_l2, w_o=w_o.astype(BF16),
                mu_rkv=jnp.stack([mu[0], mu[2], mu[3]]), mu_l1=jnp.stack([mu[5], mu[1], mu[4]]),
                w0=w0[:, None, :], a0=a0[:, None, :], k_k=k_k[None], k_a=k_a[None],
                r_k=r_k.reshape(1, D), ln_w=ln_w[None], ln_b=ln_b[None])


def _tile(n, pref):
    t = min(pref, n)
    while n % t:
        t //= 2
    return t


def attention_layer(x, B, S, g_pre, g_post, ap, bias_t, cos_t, sin_t, H):
    T, D = x.shape
    tm = _tile(T, 512)
    qr = ap["q_norm"].shape[1]
    n_lat = ap["w_lat"].shape[1]
    lat = fused_mm([x], [g_pre], ap["w_lat"], _norm_prologue, _store_epilogue, out_dtype=F32,
                   tm=tm, tn=n_lat, name="attn_latent")
    n_dil = ap["w_dil"].shape[1]
    dil_scale = jnp.concatenate([jnp.full((1, n_dil // 3), DIL_HEAD_DIM ** -0.5 * LOG2E, F32),
                                 jnp.ones((1, 2 * n_dil // 3), F32)], axis=1)
    qkv_b = fused_mm([x], [g_pre], ap["w_dil"], _norm_prologue, _scale_epilogue, out_dtype=BF16,
                     tm=tm, tn=n_dil // 3,
                     extras=[(dil_scale, pl.BlockSpec((1, n_dil // 3), lambda i, j: (0, j)))],
                     name="attn_dil_qkv")
    q_scale = (MLA_NOPE_DIM + MLA_ROPE_DIM) ** -0.5 * LOG2E
    nq = ap["w_q"].shape[1]
    tab = pl.BlockSpec((tm, LANES), lambda i, j: (i, 0))
    q_all = fused_mm([lat], [ap["q_norm"]], ap["w_q"], _norm_prologue,
                     functools.partial(_q_epilogue, scale=q_scale), out_dtype=BF16, tm=tm,
                     tn=nq // 2, row_cols=[0], row_width=qr, extras=[(cos_t, tab), (sin_t, tab)],
                     name="mla_q")
    kv = fused_mm([lat], [ap["kv_norm"]], ap["w_kv"], _norm_prologue, _store_epilogue,
                  out_dtype=BF16, tm=tm, tn=ap["w_kv"].shape[1] // 2, row_cols=[1], row_width=qr,
                  name="mla_kv")
    krope = rope_k(lat, (n_lat - LANES) // LANES, cos_t, sin_t, tm=tm)
    a_out = mla_attention(q_all, kv, krope, B, S, H, tq=MXU_DIM, n_chain=2, tk=_tile(S, 1024))
    b_out = dilated_attention(qkv_b, bias_t, B, S, H, t=bias_t.shape[-1])
    return mm_post(a_out, b_out, ap["w_out"], g_post, x, tm=_tile(T, 256), name="attn_out")


def rwkv_layer(x, B, S, g_pre, g_post, rp):
    T, D = x.shape
    x3 = x.reshape(B, S, D)
    zero = jnp.zeros((B, 1, D), F32)
    xp = jnp.concatenate([zero, x3[:, :-1]], axis=1).reshape(T, D)
    xn = jnp.concatenate([x3[:, 1:], zero], axis=1).reshape(T, D)
    tm = _tile(T, 256)
    rkv = fused_mm([x, xp, xn], [g_pre, rp["mu_rkv"]], rp["w_rkv"], _shift_prologue,
                   _store_epilogue, out_dtype=F32, tm=tm, tn=D // 2, n_mix=3, tiles_per_mix=2,
                   name="rwkv_rkv")
    l1 = fused_mm([x, xp, xn], [g_pre, rp["mu_l1"]], rp["w_l1"], _shift_prologue,
                  _lora1_epilogue, out_dtype=BF16, tm=tm, tn=MXU_DIM, n_mix=3, tiles_per_mix=1,
                  name="rwkv_lora1")
    l2 = lora2(l1, rp["w_l2"], tm=_tile(T, 512), tn=D // 2 if D >= 512 else D, kb=MXU_DIM)
    y = rwkv_scan(rkv, l2, rp["w0"], rp["a0"], rp["k_k"], rp["k_a"], B, S, D)
    return rwkv_out(y, rkv, l2, rp["r_k"], rp["ln_w"], rp["ln_b"], rp["w_o"], g_post, x,
                    tm=_tile(T, 256))


def trunk(x3, p):
    B, S, D = x3.shape
    x = x3.reshape(B * S, D)
    depth = p["norm_g"].shape[0]
    for layer in range(depth):
        i = layer // 2
        g = p["norm_g"][layer]
        if layer % 2 == 0:
            x = attention_layer(x, B, S, g[0][None], g[1][None], p["attn"][i], p["bias_t"],
                                p["cos_t"][(B, S)], p["sin_t"][(B, S)], p["heads"])
        else:
            x = rwkv_layer(x, B, S, g[0][None], g[1][None], p["rwkv"][i])
        x = ffn(x, g[2][None], g[3][None], p["ffn_wg"][layer], p["ffn_wu"][layer],
                p["ffn_wd"][layer], tm=_tile(B * S, 512), tf=_tile(p["ffn_wg"][layer].shape[1], 512))
    return x.reshape(B, S, D)


def kernel(x_prompt, x_sample, norm_g, rel_bias, at_w_in, at_q_norm, at_kv_norm, at_w_uq, at_w_ukv, at_w_out, rw_mu, rw_w_r, rw_w_k, rw_w_v, rw_w_o, rw_w0, rw_w1, rw_w2, rw_a0, rw_a1, rw_a2, rw_g1, rw_g2, rw_k_k, rw_k_a, rw_r_k, rw_ln_w, rw_ln_b, ffn_w_gate, ffn_w_up, ffn_w_down):
    H = rel_bias.shape[1]
    p = {"norm_g": norm_g, "heads": H}
    p["attn"] = [_attn_params(at_w_in[i], at_q_norm[i], at_kv_norm[i], at_w_uq[i], at_w_ukv[i],
                              at_w_out[i], H) for i in range(at_w_in.shape[0])]
    p["rwkv"] = [_rwkv_params(rw_mu[i], rw_w_r[i], rw_w_k[i], rw_w_v[i], rw_w_o[i], rw_w0[i],
                              rw_w1[i], rw_w2[i], rw_a0[i], rw_a1[i], rw_a2[i], rw_g1[i],
                              rw_g2[i], rw_k_k[i], rw_k_a[i], rw_r_k[i], rw_ln_w[i], rw_ln_b[i])
                 for i in range(rw_mu.shape[0])]
    p["ffn_wg"] = ffn_w_gate.astype(BF16)
    p["ffn_wu"] = ffn_w_up.astype(BF16)
    p["ffn_wd"] = ffn_w_down.astype(BF16)
    p["bias_t"] = dilated_bias_tiles(rel_bias, MXU_DIM)
    p["cos_t"], p["sin_t"] = {}, {}
    for xs in (x_prompt, x_sample):
        B, S = xs.shape[:2]
        p["cos_t"][(B, S)], p["sin_t"][(B, S)] = _rope_tables(B, S)
    return trunk(x_prompt, p), trunk(x_sample, p)
```

```python
import functools
import math

import jax
import jax.numpy as jnp
import numpy as np
from jax import lax
from jax.experimental import pallas as pl
from jax.experimental.pallas import tpu as pltpu

F32 = jnp.float32
BF16 = jnp.bfloat16

MLA_NOPE_DIM = 128
MLA_ROPE_DIM = 64
MLA_V_DIM = 128
ROPE_BASE = 10000.0
DIL_HEAD_DIM = 128
DIL_PATTERNS = ((128, 1), (512, 4), (2048, 16))
N_BUCKETS = 32
T5_MAX_DISTANCE = 1024
RWKV_HEAD = 64
RWKV_LN_EPS = 64e-5
NORM_EPS = 1e-6
NEG_BIG = -1e30
LOG2E = 1.4426950408889634

LANES = 128
SUBLANES = 8
MXU_DIM = 256
VMEM_LIMIT = 56 * 1024 * 1024

SCAN_CHUNK = 64
SCAN_LANES = 2048


def _cparams(sem, vmem=VMEM_LIMIT):
    return pltpu.CompilerParams(dimension_semantics=sem, vmem_limit_bytes=vmem)


def _rms(x, g):
    return x * lax.rsqrt(jnp.mean(x * x, axis=-1, keepdims=True) + NORM_EPS) * g


def _dot(a, b):
    return jnp.dot(a, b, preferred_element_type=F32)


def _dot_nt(a, b):
    return lax.dot_general(a, b, (((1,), (1,)), ((), ())), preferred_element_type=F32)


def _dot_tn(a, b):
    return lax.dot_general(a, b, (((0,), (0,)), ((), ())), preferred_element_type=F32)


def _sigmoid(x):
    return 1.0 / (1.0 + jnp.exp(-x))


def _fused_mm_kernel(*refs, n_rows, n_vecs, n_extra, n_mix, tiles_per_mix, tiles_per_seq,
                     prologue, epilogue):
    rows = refs[:n_rows]
    vecs = refs[n_rows:n_rows + n_vecs]
    w_ref = refs[n_rows + n_vecs]
    base = n_rows + n_vecs + 1
    extras = refs[base:base + n_extra]
    o_ref = refs[base + n_extra]
    h_ref = refs[base + n_extra + 1]
    j = pl.program_id(1)
    flags = ()
    if tiles_per_seq:
        it = pl.program_id(0) % tiles_per_seq
        flags = (jnp.where(it == 0, 0.0, 1.0), jnp.where(it == tiles_per_seq - 1, 0.0, 1.0))

    @pl.when(j == 0)
    def _():
        hs = prologue(*[r[...] for r in rows], *flags, *[v[...] for v in vecs])
        for m in range(n_mix):
            h_ref[m] = hs[m].astype(BF16)

    if n_mix == 1:
        h = h_ref[0]
    else:
        h = h_ref[j // tiles_per_mix]
    acc = _dot(h, w_ref[...])
    epilogue(j, acc, o_ref, *extras)


def fused_mm(rows, vecs, w, prologue, epilogue, *, out_dtype, tm, tn, row_cols=None,
             row_width=None, n_mix=1, tiles_per_mix=1, extras=(), seq_len=None, name):
    T = rows[0].shape[0]
    K, N = w.shape
    row_cols = row_cols or [0] * len(rows)
    row_width = row_width or K
    assert T % tm == 0 and N % tn == 0, (T, tm, N, tn)
    in_specs = [pl.BlockSpec((tm, row_width), functools.partial(lambda i, j, c: (i, c), c=c))
                for c in row_cols]
    if seq_len:
        assert len(rows) == 1 and seq_len % tm == 0
        g8, last8 = tm // SUBLANES, T // SUBLANES - 1
        rows = [rows[0]] * 3
        in_specs += [pl.BlockSpec((SUBLANES, K), lambda i, j: (jnp.maximum(i * g8 - 1, 0), 0)),
                     pl.BlockSpec((SUBLANES, K), lambda i, j: (jnp.minimum((i + 1) * g8, last8), 0))]
    in_specs += [pl.BlockSpec(v.shape, lambda i, j: (0, 0)) for v in vecs]
    in_specs += [pl.BlockSpec((K, tn), lambda i, j: (0, j))]
    in_specs += [spec for _, spec in extras]
    kern = functools.partial(
        _fused_mm_kernel, n_rows=len(rows), n_vecs=len(vecs), n_extra=len(extras), n_mix=n_mix,
        tiles_per_mix=tiles_per_mix, tiles_per_seq=seq_len // tm if seq_len else 0,
        prologue=prologue, epilogue=epilogue)
    return pl.pallas_call(
        kern,
        grid=(T // tm, N // tn),
        in_specs=in_specs,
        out_specs=pl.BlockSpec((tm, tn), lambda i, j: (i, j)),
        out_shape=jax.ShapeDtypeStruct((T, N), out_dtype),
        scratch_shapes=[pltpu.VMEM((n_mix, tm, K), BF16)],
        compiler_params=_cparams(("parallel", "arbitrary")),
        name=name,
    )(*rows, *vecs, w, *[a for a, _ in extras])


def _norm_prologue(x, g):
    return (_rms(x, g),)


def _store_epilogue(j, acc, o_ref):
    o_ref[...] = acc.astype(o_ref.dtype)


def _scale_epilogue(j, acc, o_ref, s_ref):
    o_ref[...] = (acc * s_ref[...]).astype(o_ref.dtype)


def _mm_post_kernel(a_ref, b_ref, w_ref, g_ref, x_ref, o_ref):
    lhs = jnp.concatenate([a_ref[...], b_ref[...]], axis=-1)
    y = _dot(lhs, w_ref[...])
    o_ref[...] = x_ref[...] + _rms(y, g_ref[...])


def mm_post(a, b, w, g, x, *, tm, name):
    T, Ka = a.shape
    Kb = b.shape[1]
    D = w.shape[1]
    return pl.pallas_call(
        _mm_post_kernel,
        grid=(T // tm,),
        in_specs=[pl.BlockSpec((tm, Ka), lambda i: (i, 0)),
                  pl.BlockSpec((tm, Kb), lambda i: (i, 0)),
                  pl.BlockSpec((Ka + Kb, D), lambda i: (0, 0), pipeline_mode=pl.Buffered(1)),
                  pl.BlockSpec((1, D), lambda i: (0, 0)),
                  pl.BlockSpec((tm, D), lambda i: (i, 0))],
        out_specs=pl.BlockSpec((tm, D), lambda i: (i, 0)),
        out_shape=jax.ShapeDtypeStruct((T, D), F32),
        compiler_params=_cparams(("parallel",)),
        name=name,
    )(a, b, w, g, x)


def _ffn_kernel(x_ref, gpre_ref, gpost_ref, wg_ref, wu_ref, wd_ref, o_ref, h_ref, acc_ref):
    f = pl.program_id(1)

    @pl.when(f == 0)
    def _():
        h_ref[...] = _rms(x_ref[...], gpre_ref[...]).astype(BF16)
        acc_ref[...] = jnp.zeros_like(acc_ref)

    h = h_ref[...]
    gate = _dot(h, wg_ref[...])
    up = _dot(h, wu_ref[...])
    act = (gate * _sigmoid(gate) * up).astype(BF16)
    acc_ref[...] += _dot(act, wd_ref[...])

    @pl.when(f == pl.num_programs(1) - 1)
    def _():
        o_ref[...] = x_ref[...] + _rms(acc_ref[...], gpost_ref[...])


def ffn(x, g_pre, g_post, wg, wu, wd, *, tm, tf):
    T, D = x.shape
    Fh = wg.shape[1]
    assert T % tm == 0 and Fh % tf == 0
    return pl.pallas_call(
        _ffn_kernel,
        grid=(T // tm, Fh // tf),
        in_specs=[pl.BlockSpec((tm, D), lambda i, f: (i, 0)),
                  pl.BlockSpec((1, D), lambda i, f: (0, 0)),
                  pl.BlockSpec((1, D), lambda i, f: (0, 0)),
                  pl.BlockSpec((D, tf), lambda i, f: (0, f)),
                  pl.BlockSpec((D, tf), lambda i, f: (0, f)),
                  pl.BlockSpec((tf, D), lambda i, f: (f, 0))],
        out_specs=pl.BlockSpec((tm, D), lambda i, f: (i, 0)),
        out_shape=jax.ShapeDtypeStruct((T, D), F32),
        scratch_shapes=[pltpu.VMEM((tm, D), BF16), pltpu.VMEM((tm, D), F32)],
        compiler_params=_cparams(("parallel", "arbitrary")),
        name="ffn",
    )(x, g_pre, g_post, wg, wu, wd)


def _rope_groups(x, cos_t, sin_t):
    half = MLA_ROPE_DIM // 2
    n = x.shape[-1] // LANES
    lane = lax.broadcasted_iota(jnp.int32, x.shape, 1) % LANES
    partner = jnp.where(lane < half, pltpu.roll(x, x.shape[-1] - half, 1), pltpu.roll(x, half, 1))
    if n > 1:
        cos_t = jnp.concatenate([cos_t] * n, axis=-1)
        sin_t = jnp.concatenate([sin_t] * n, axis=-1)
    return x * cos_t + partner * sin_t


def _q_epilogue(j, acc, o_ref, cos_ref, sin_ref, *, scale):
    @pl.when(j == 0)
    def _():
        o_ref[...] = (acc * scale).astype(o_ref.dtype)

    @pl.when(j == 1)
    def _():
        o_ref[...] = (_rope_groups(acc, cos_ref[...], sin_ref[...]) * scale).astype(o_ref.dtype)


def _rope_k_kernel(x_ref, cos_ref, sin_ref, o_ref):
    o_ref[...] = _rope_groups(x_ref[...], cos_ref[...], sin_ref[...]).astype(o_ref.dtype)


def rope_k(lat, col_block, cos_t, sin_t, *, tm):
    T = lat.shape[0]
    return pl.pallas_call(
        _rope_k_kernel,
        grid=(T // tm,),
        in_specs=[pl.BlockSpec((tm, LANES), lambda i: (i, col_block)),
                  pl.BlockSpec((tm, LANES), lambda i: (i, 0)),
                  pl.BlockSpec((tm, LANES), lambda i: (i, 0))],
        out_specs=pl.BlockSpec((tm, LANES), lambda i: (i, 0)),
        out_shape=jax.ShapeDtypeStruct((T, LANES), BF16),
        compiler_params=_cparams(("parallel",)),
        name="rope_k",
    )(lat, cos_t, sin_t)


def _softmax_weights(ss, m_prev):
    m_new, ps = [], []
    for h, tiles in enumerate(ss):
        m = m_prev[h]
        for s in tiles:
            m = jnp.maximum(m, jnp.max(s, axis=0, keepdims=True))
        m_new.append(m)
    for h, tiles in enumerate(ss):
        ps.append([jnp.exp2(s - m_new[h]) for s in tiles])
    return m_new, ps


def _weighted_values(ps, vs, l, acc):
    for p, v in zip(ps, vs):
        l = l + jnp.sum(p, axis=0, keepdims=True)
        acc = acc + _dot_tn(v, p.astype(BF16))
    return l, acc


def _mla_kernel(qn_ref, qr_ref, kn_ref, kr_ref, v_ref, o_ref, m_ref, l_ref, acc_ref, sa_ref,
                sb_ref, *, tk):
    nkv = kn_ref.shape[0] // tk
    assert nkv % 2 == 0
    n, _, tq = acc_ref.shape
    q = jnp.concatenate([qn_ref[...], qr_ref[...]], axis=-1)
    qs = [q[h * tq:(h + 1) * tq] for h in range(n)]
    m_ref[...] = jnp.full_like(m_ref, NEG_BIG)
    l_ref[...] = jnp.zeros_like(l_ref)
    acc_ref[...] = jnp.zeros_like(acc_ref)

    def chunk(c):
        return pl.ds(c * tk if isinstance(c, int) else pl.multiple_of(c * tk, tk), tk)

    def put_scores(c, s_ref):
        k = jnp.concatenate([kn_ref[chunk(c), :], kr_ref[chunk(c), :]], axis=-1)
        for h in range(n):
            s_ref[h] = _dot_nt(k, qs[h])

    def consume(c, s_ref):
        v = v_ref[chunk(c), :]
        m_prev = [m_ref[h] for h in range(n)]
        m_new, ps = _softmax_weights([[s_ref[h]] for h in range(n)], m_prev)
        for h in range(n):
            alpha = jnp.exp2(m_prev[h] - m_new[h])
            l, acc = _weighted_values(ps[h], [v], alpha * l_ref[h], alpha * acc_ref[h])
            l_ref[h] = l
            acc_ref[h] = acc
            m_ref[h] = m_new[h]

    put_scores(0, sa_ref)

    def body(i, carry):
        c = 2 * i
        put_scores(c + 1, sb_ref)
        consume(c, sa_ref)
        put_scores(c + 2, sa_ref)
        consume(c + 1, sb_ref)
        return carry

    lax.fori_loop(0, nkv // 2 - 1, body, 0)
    put_scores(nkv - 1, sb_ref)
    consume(nkv - 2, sa_ref)
    consume(nkv - 1, sb_ref)
    for h in range(n):
        o_ref[h * tq:(h + 1) * tq, :] = (acc_ref[h] / l_ref[h]).T.astype(o_ref.dtype)


def mla_attention(q_all, kv, krope, B, S, H, *, tq, n_chain, tk):
    T = B * S
    tb = tq * n_chain
    nq = S // tb
    dv = MLA_V_DIM
    return pl.pallas_call(
        functools.partial(_mla_kernel, tk=tk),
        grid=(B, H, nq),
        in_specs=[pl.BlockSpec((tb, LANES), lambda b, h, i: (b * nq + i, h)),
                  pl.BlockSpec((tb, LANES), lambda b, h, i: (b * nq + i, H + h)),
                  pl.BlockSpec((S, LANES), lambda b, h, i: (b, h)),
                  pl.BlockSpec((S, LANES), lambda b, h, i: (b, 0)),
                  pl.BlockSpec((S, LANES), lambda b, h, i: (b, H + h))],
        out_specs=pl.BlockSpec((tb, dv), lambda b, h, i: (b * nq + i, h)),
        out_shape=jax.ShapeDtypeStruct((T, H * dv), BF16),
        scratch_shapes=[pltpu.VMEM((n_chain, 1, tq), F32), pltpu.VMEM((n_chain, 1, tq), F32),
                        pltpu.VMEM((n_chain, dv, tq), F32), pltpu.VMEM((n_chain, tk, tq), F32),
                        pltpu.VMEM((n_chain, tk, tq), F32)],
        compiler_params=_cparams(("parallel", "parallel", "arbitrary")),
        name="mla_attention",
    )(q_all, q_all, kv, krope, kv)


def _dil_kernel(q_ref, k_ref, v_ref, bias_ref, o_ref, *, t, nside):
    nk = k_ref.shape[0] // t
    i = pl.program_id(2)
    q = q_ref[...]
    ss, vs = [], []
    for d in range(2 * nside + 1):
        c = i + d - nside
        inside = jnp.logical_and(c >= 0, c < nk)
        rows = pl.ds(pl.multiple_of(jnp.clip(c, 0, nk - 1) * t, t), t)
        ss.append(_dot_nt(k_ref[rows, :], q) + (bias_ref[d] + jnp.where(inside, 0.0, NEG_BIG)))
        vs.append(v_ref[rows, :])
    m_new, ps = _softmax_weights([ss], [jnp.full((1, t), NEG_BIG, F32)])
    l, acc = _weighted_values(ps[0], vs, jnp.zeros((1, t), F32), jnp.zeros((DIL_HEAD_DIM, t), F32))
    o_ref[...] = (acc / l).T.astype(o_ref.dtype)


def dilated_attention(qkv, bias_t, B, S, H, *, t):
    T = B * S
    nq = S // t
    nd = bias_t.shape[0]
    dh = DIL_HEAD_DIM
    return pl.pallas_call(
        functools.partial(_dil_kernel, t=t, nside=(nd - 1) // 2),
        grid=(B, H, nq),
        in_specs=[pl.BlockSpec((t, dh), lambda b, h, i: (b * nq + i, h)),
                  pl.BlockSpec((S, dh), lambda b, h, i: (b, H + h)),
                  pl.BlockSpec((S, dh), lambda b, h, i: (b, 2 * H + h)),
                  pl.BlockSpec((nd, None, t, t), lambda b, h, i: (0, h, 0, 0))],
        out_specs=pl.BlockSpec((t, dh), lambda b, h, i: (b * nq + i, h)),
        out_shape=jax.ShapeDtypeStruct((T, H * dh), BF16),
        compiler_params=_cparams(("parallel", "parallel", "arbitrary")),
        name="dilated_attention",
    )(qkv, qkv, qkv, bias_t)


def _t5_bucket_np(rel):
    half = N_BUCKETS // 2
    max_exact = half // 2
    bucket = np.where(rel > 0, half, 0)
    n = np.abs(rel)
    nf = np.maximum(n, 1).astype(np.float64)
    large = max_exact + (np.log(nf / max_exact) / math.log(T5_MAX_DISTANCE / max_exact)
                         * (half - max_exact)).astype(np.int64)
    large = np.minimum(large, half - 1)
    return bucket + np.where(n < max_exact, n, large)


def dilated_bias_tiles(rel_bias, t):
    reach = max(w // 2 for w, _ in DIL_PATTERNS)
    nside = -(-reach // t)
    nd = 2 * nside + 1
    span = (nside + 1) * t
    delta = np.arange(-span + 1, span)
    mult = np.zeros(delta.shape, np.int64)
    for window, dil in DIL_PATTERNS:
        mult += ((delta % dil == 0) & (np.abs(delta) <= window // 2)).astype(np.int64)
    logm = np.log(np.maximum(mult, 1)).astype(np.float32)
    f = (rel_bias.astype(F32)[_t5_bucket_np(delta)] + logm[:, None]) * LOG2E
    f = jnp.where((mult > 0)[:, None], f, NEG_BIG)
    H = f.shape[1]
    frev = jnp.pad(f[::-1].T, ((0, 0), (1, 0)))
    wins = []
    for d in range(nd):
        s_d = span - 1 - (d - nside) * t + 1
        wins.append(jnp.concatenate([frev[:, s_d:s_d + t], frev[:, s_d - t:s_d]], axis=1))
    g = jnp.stack(wins, axis=0)
    flat = jnp.tile(g, (1, 1, t))[:, :, :t * (2 * t - 1)]
    return flat.reshape(nd, H, t, 2 * t - 1)[:, :, :, :t]


def _shift_prologue(x, x_before, x_after, keep_before, keep_after, g, mu):
    tm = x.shape[0]
    h = _rms(x, g)
    edge_prev = _rms(x_before[SUBLANES - 1:SUBLANES, :], g) * keep_before
    edge_next = _rms(x_after[0:1, :], g) * keep_after
    row = lax.broadcasted_iota(jnp.int32, h.shape, 0)
    h_prev = jnp.where(row == 0, edge_prev, pltpu.roll(h, 1, 0))
    h_next = jnp.where(row == tm - 1, edge_next, pltpu.roll(h, tm - 1, 0))
    xx = 0.5 * (h_prev + h_next) - h
    return tuple(h + xx * mu[m:m + 1, :] for m in range(mu.shape[0]))


def _lora1_epilogue(j, acc, o_ref):
    @pl.when(j == 0)
    def _():
        o_ref[...] = _sigmoid(acc).astype(o_ref.dtype)

    @pl.when(j == 1)
    def _():
        o_ref[...] = jnp.tanh(acc).astype(o_ref.dtype)

    @pl.when(j == 2)
    def _():
        o_ref[...] = acc.astype(o_ref.dtype)


def _lora2_kernel(a_ref, w_ref, o_ref):
    o_ref[...] = _dot(a_ref[...], w_ref[...])


def lora2(l1, w, *, tm, tn, kb):
    T = l1.shape[0]
    N = w.shape[1]
    D = N // 5

    def a_map(i, j):
        return (i, (j * tn + D) // (2 * D))

    return pl.pallas_call(
        _lora2_kernel,
        grid=(T // tm, N // tn),
        in_specs=[pl.BlockSpec((tm, kb), a_map), pl.BlockSpec((kb, tn), lambda i, j: (0, j))],
        out_specs=pl.BlockSpec((tm, tn), lambda i, j: (i, j)),
        out_shape=jax.ShapeDtypeStruct((T, N), F32),
        compiler_params=_cparams(("parallel", "parallel")),
        name="rwkv_lora2",
    )(l1, w)


def _head_block_mask(n):
    r = lax.broadcasted_iota(jnp.int32, (n, n), 0) // RWKV_HEAD
    c = lax.broadcasted_iota(jnp.int32, (n, n), 1) // RWKV_HEAD
    return r == c


def _head_sums(x):
    ones = jnp.where(_head_block_mask(MXU_DIM), 1.0, 0.0).astype(BF16)
    parts = [_dot(x[:, g:g + MXU_DIM].astype(BF16), ones) for g in range(0, x.shape[-1], MXU_DIM)]
    return parts[0] if len(parts) == 1 else jnp.concatenate(parts, axis=-1)


def _split3(x):
    hi = x.astype(BF16)
    r1 = x - hi.astype(F32)
    mid = r1.astype(BF16)
    lo = (r1 - mid.astype(F32)).astype(BF16)
    return hi, mid, lo


def _scan_kernel(r_ref, k_ref, v_ref, wl_ref, al_ref, w0_ref, a0_ref, kk_ref, ka_ref, y_ref,
                 s_ref):
    C = SCAN_CHUNK
    HN = RWKV_HEAD
    d = pl.program_id(1)
    c = pl.program_id(3)
    fwd = d == 0

    @pl.when(c == 0)
    def _():
        s_ref[...] = jnp.zeros_like(s_ref)

    r = r_ref[...]
    k = k_ref[...]
    v = v_ref[...]
    kk = k * kk_ref[...]
    kk = kk * lax.rsqrt(_head_sums(kk * kk) + 1e-12)
    z = w0_ref[...] + wl_ref[...]
    w_log = -(jnp.maximum(-z, 0.0) + jnp.log(1.0 + jnp.exp(-jnp.abs(z)))) - 0.5
    lw = -jnp.exp(w_log)
    a = _sigmoid(a0_ref[...] + al_ref[...])
    kd = k * (1.0 + (a - 1.0) * ka_ref[...])
    aa = -kk
    bb = kk * a

    sign = 1 - 2 * d
    trow = lax.broadcasted_iota(jnp.int32, (C, C), 0)
    tcol = lax.broadcasted_iota(jnp.int32, (C, C), 1)
    cum = jnp.where((trow - tcol) * sign >= 0, 1.0, 0.0).astype(BF16)
    hi, mid, lo = _split3(lw)
    l_inc = _dot(cum, hi) + _dot(cum, mid) + _dot(cum, lo)
    l_exc = l_inc - lw
    l_ref = l_inc[C // 2:C // 2 + 1, :]
    l_end = jnp.where(fwd, l_inc[C - 1:C, :], l_inc[0:1, :])
    e_in = jnp.exp(l_inc - l_ref)
    e_ex = jnp.exp(l_exc - l_ref)
    e_ng = jnp.exp(l_ref - l_inc)
    rho = jnp.exp(l_ref)
    p_end = jnp.exp(l_end)
    end_over_ref = jnp.exp(l_end - l_ref)
    at = aa * e_ex
    rt = r * e_in
    bt = bb * e_ng
    kt = kd * e_ng
    at_s = at * rho
    rt_s = rt * rho
    bh = bt * end_over_ref
    kh = kt * end_over_ref

    lane = lax.broadcasted_iota(jnp.int32, (1, LANES), 1)
    first = lane < HN

    def mstack(x):
        zero = jnp.zeros_like(x)
        return jnp.concatenate([jnp.where(first, x, zero), jnp.where(first, zero, x)], axis=0)

    trow = lax.broadcasted_iota(jnp.int32, (C, LANES), 0)
    scol = lax.broadcasted_iota(jnp.int32, (C, LANES), 1) % HN
    strict = (trow - scol) * sign > 0
    incl = (trow - scol) * sign >= 0
    eye = jnp.where(scol == trow, 1.0, 0.0)
    bdiag = _head_block_mask(LANES)

    groups = [slice(p * LANES, (p + 1) * LANES) for p in range(r.shape[-1] // LANES)]
    cat = jnp.concatenate
    v_b = [v[:, sl].astype(BF16) for sl in groups]
    sc = [_dot_nt(cat([at[:, sl], rt[:, sl]], axis=0).astype(BF16),
                  cat([mstack(bt[:, sl].astype(BF16)), mstack(kt[:, sl].astype(BF16))], axis=0))
          for sl in groups]
    a_ab = [jnp.where(strict, s[:C, :LANES], 0.0) for s in sc]
    a_ak = [jnp.where(strict, s[:C, LANES:], 0.0) for s in sc]
    a_r = [cat([jnp.where(incl, s[C:, :LANES], 0.0), jnp.where(incl, s[C:, LANES:], 0.0)],
               axis=1).astype(BF16) for s in sc]

    xb = [x.astype(BF16) for x in a_ab]
    tinv = [eye + x for x in a_ab]
    xp = [_dot(x, mstack(x)) for x in xb]
    av = [_dot(x.astype(BF16), mstack(vb)) for x, vb in zip(a_ak, v_b)]
    for _ in range(int(math.log2(C)) - 2):
        xb = [x.astype(BF16) for x in xp]
        both = [_dot(cat([t.astype(BF16), x], axis=0), mstack(x)) for t, x in zip(tinv, xb)]
        tinv = [t + b[:C] for t, b in zip(tinv, both)]
        xp = [b[C:] for b in both]
    tinv = [t + _dot(t.astype(BF16), mstack(x.astype(BF16))) for t, x in zip(tinv, xp)]

    uw = [_dot(t.astype(BF16),
               cat([mstack(x.astype(BF16)), mstack(at_s[:, sl].astype(BF16))], axis=1))
          for t, x, sl in zip(tinv, av, groups)]

    s_old = [s_ref[p] for p in range(len(groups))]
    us = [_dot_nt(cat([w[:, LANES:], rt_s[:, sl]], axis=0).astype(BF16), s.astype(BF16))
          for w, sl, s in zip(uw, groups, s_old)]
    u = [x[:C] + w[:, :LANES] for x, w in zip(us, uw)]
    y = [x[C:] + _dot(ar, cat([mstack(uu.astype(BF16)), mstack(vb)], axis=0))
         for x, ar, uu, vb in zip(us, a_r, u, v_b)]
    for sl, yy in zip(groups, y):
        y_ref[:, sl] = yy

    s_new = [_dot(cat([uu, v[:, sl]], axis=0).T.astype(BF16),
                  cat([bh[:, sl], kh[:, sl]], axis=0).astype(BF16))
             for uu, sl in zip(u, groups)]
    for p, sl in enumerate(groups):
        s_ref[p] = jnp.where(bdiag, s_new[p], 0.0) + s_old[p] * p_end[:, sl]


def rwkv_scan(rkv, l2, w0, a0, k_k, k_a, B, S, D):
    T = B * S
    C = SCAN_CHUNK
    L = min(SCAN_LANES, D)
    nc = S // C
    ng = D // L

    def row(b, d, c):
        return b * nc + c + d * (nc - 1 - 2 * c)

    def rspec(off):
        return pl.BlockSpec((C, L), lambda b, d, g, c: (row(b, d, c), off * ng + g))

    def dspec(off):
        return pl.BlockSpec((C, L), lambda b, d, g, c: (row(b, d, c), (off + d) * ng + g))

    vdir = pl.BlockSpec((None, 1, L), lambda b, d, g, c: (d, 0, g))
    vec = pl.BlockSpec((1, L), lambda b, d, g, c: (0, g))
    return pl.pallas_call(
        _scan_kernel,
        grid=(B, 2, ng, nc),
        in_specs=[rspec(0), rspec(1), rspec(2), dspec(1), dspec(3), vdir, vdir, vec, vec],
        out_specs=pl.BlockSpec((None, C, L), lambda b, d, g, c: (d, row(b, d, c), g)),
        out_shape=jax.ShapeDtypeStruct((2, T, D), F32),
        scratch_shapes=[pltpu.VMEM((L // LANES, LANES, LANES), F32)],
        compiler_params=_cparams(("parallel", "parallel", "parallel", "arbitrary")),
        name="rwkv_scan",
    )(rkv, rkv, rkv, l2, l2, w0, a0, k_k, k_a)


def _rwkv_out_kernel(y_ref, r_ref, k_ref, v_ref, g_ref, rk_ref, lnw_ref, lnb_ref, wo_ref,
                     gpost_ref, x_ref, o_ref):
    inv_n = 1.0 / RWKV_HEAD
    y = y_ref[0] + y_ref[1]
    mean = _head_sums(y) * inv_n
    yc = y - mean
    var = _head_sums(yc * yc) * inv_n
    yn = yc * lax.rsqrt(var + RWKV_LN_EPS) * lnw_ref[...] + lnb_ref[...]
    bonus = _head_sums(r_ref[...] * k_ref[...] * rk_ref[...]) * v_ref[...]
    mix = ((yn + bonus) * g_ref[...]).astype(BF16)
    out = _dot(mix, wo_ref[...])
    o_ref[...] = x_ref[...] + _rms(out, gpost_ref[...])


def rwkv_out(y, rkv, l2, r_k, ln_w, ln_b, w_o, g_post, x, *, tm):
    T, D = x.shape
    vec = pl.BlockSpec((1, D), lambda i: (0, 0))
    return pl.pallas_call(
        _rwkv_out_kernel,
        grid=(T // tm,),
        in_specs=[pl.BlockSpec((2, tm, D), lambda i: (0, i, 0)),
                  pl.BlockSpec((tm, D), lambda i: (i, 0)),
                  pl.BlockSpec((tm, D), lambda i: (i, 1)),
                  pl.BlockSpec((tm, D), lambda i: (i, 2)),
                  pl.BlockSpec((tm, D), lambda i: (i, 0)),
                  vec, vec, vec,
                  pl.BlockSpec((D, D), lambda i: (0, 0), pipeline_mode=pl.Buffered(1)),
                  vec,
                  pl.BlockSpec((tm, D), lambda i: (i, 0))],
        out_specs=pl.BlockSpec((tm, D), lambda i: (i, 0)),
        out_shape=jax.ShapeDtypeStruct((T, D), F32),
        compiler_params=_cparams(("parallel",)),
        name="rwkv_out",
    )(y, rkv, rkv, rkv, l2, r_k, ln_w, ln_b, w_o, g_post, x)


def _pad_cols(w, n):
    return jnp.pad(w, ((0, 0), (0, n - w.shape[1])))


def _pad_rows(w, n):
    return jnp.pad(w, ((0, n - w.shape[0]), (0, 0)))


def _attn_params(w_in, q_norm, kv_norm, w_uq, w_ukv, w_out, H):
    qr, kvr = q_norm.shape[0], kv_norm.shape[0]
    rope, nope, vd = MLA_ROPE_DIM, MLA_NOPE_DIM, MLA_V_DIM
    w_lat = jnp.concatenate([w_in[:, :qr + kvr], _pad_cols(w_in[:, qr + kvr:qr + kvr + rope], LANES)],
                            axis=1).astype(BF16)
    w_dil = w_in[:, qr + kvr + rope:].astype(BF16)
    uq = w_uq.reshape(qr, H, nope + rope)
    uq_rope = jnp.pad(uq[:, :, nope:], ((0, 0), (0, 0), (0, LANES - rope)))
    w_q = jnp.concatenate([uq[:, :, :nope].reshape(qr, H * nope), uq_rope.reshape(qr, H * LANES)],
                          axis=1).astype(BF16)
    ukv = w_ukv.reshape(kvr, H, nope + vd)
    w_kv = jnp.concatenate([ukv[:, :, :nope].reshape(kvr, H * nope),
                            ukv[:, :, nope:].reshape(kvr, H * vd)], axis=1).astype(BF16)
    return dict(w_lat=w_lat, w_dil=w_dil, w_q=w_q, w_kv=w_kv, w_out=w_out.astype(BF16),
                q_norm=q_norm[None], kv_norm=kv_norm[None])


def _rope_tables(B, S):
    half = MLA_ROPE_DIM // 2
    inv = ROPE_BASE ** (-jnp.arange(0, MLA_ROPE_DIM, 2, dtype=F32) / MLA_ROPE_DIM)
    ang = jnp.arange(S, dtype=F32)[:, None] * inv[None, :]
    cos, sin = jnp.cos(ang), jnp.sin(ang)
    zeros = jnp.zeros((S, LANES - 2 * half), F32)
    cos_t = jnp.concatenate([cos, cos, zeros], axis=1)
    sin_t = jnp.concatenate([-sin, sin, zeros], axis=1)
    return jnp.tile(cos_t, (B, 1)), jnp.tile(sin_t, (B, 1))


def _rwkv_params(mu, w_r, w_k, w_v, w_o, w0, w1, w2, a0, a1, a2, g1, g2, k_k, k_a, r_k, ln_w,
                 ln_b):
    D = w_r.shape[0]
    kb = MXU_DIM
    w_rkv = jnp.concatenate([w_r, w_k, w_v], axis=1).astype(BF16)
    w_l1 = jnp.concatenate([_pad_cols(g1, kb), _pad_cols(jnp.concatenate([w1[0], w1[1]], 1), kb),
                            _pad_cols(jnp.concatenate([a1[0], a1[1]], 1), kb)], axis=1).astype(BF16)
    rw = w2.shape[1]
    ra = a2.shape[1]
    z = lambda n: jnp.zeros((n, D), F32)
    w_l2 = jnp.concatenate([
        _pad_rows(g2, kb),
        _pad_rows(w2[0], kb),
        _pad_rows(jnp.concatenate([z(rw), w2[1]], 0), kb),
        _pad_rows(a2[0], kb),
        _pad_rows(jnp.concatenate([z(ra), a2[1]], 0), kb)], axis=1).astype(BF16)
    return dict(w_rkv=w_rkv, w_l1=w_l1, w_l2=w_l2, w_o=w_o.astype(BF16),
                mu_rkv=jnp.stack([mu[0], mu[2], mu[3]]), mu_l1=jnp.stack([mu[5], mu[1], mu[4]]),
                w0=w0[:, None, :], a0=a0[:, None, :], k_k=k_k[None], k_a=k_a[None],
                r_k=r_k.reshape(1, D), ln_w=ln_w[None], ln_b=ln_b[None])


def _tile(n, pref):
    t = min(pref, n)
    while n % t:
        t //= 2
    return t


def attention_layer(x, B, S, g_pre, g_post, ap, bias_t, cos_t, sin_t, H):
    T, D = x.shape
    tm = _tile(T, 512)
    qr = ap["q_norm"].shape[1]
    n_lat = ap["w_lat"].shape[1]
    lat = fused_mm([x], [g_pre], ap["w_lat"], _norm_prologue, _store_epilogue, out_dtype=F32,
                   tm=tm, tn=n_lat, name="attn_latent")
    n_dil = ap["w_dil"].shape[1]
    dil_scale = jnp.concatenate([jnp.full((1, n_dil // 3), DIL_HEAD_DIM ** -0.5 * LOG2E, F32),
                                 jnp.ones((1, 2 * n_dil // 3), F32)], axis=1)
    qkv_b = fused_mm([x], [g_pre], ap["w_dil"], _norm_prologue, _scale_epilogue, out_dtype=BF16,
                     tm=tm, tn=n_dil // 3,
                     extras=[(dil_scale, pl.BlockSpec((1, n_dil // 3), lambda i, j: (0, j)))],
                     name="attn_dil_qkv")
    q_scale = (MLA_NOPE_DIM + MLA_ROPE_DIM) ** -0.5 * LOG2E
    nq = ap["w_q"].shape[1]
    tab = pl.BlockSpec((tm, LANES), lambda i, j: (i, 0))
    q_all = fused_mm([lat], [ap["q_norm"]], ap["w_q"], _norm_prologue,
                     functools.partial(_q_epilogue, scale=q_scale), out_dtype=BF16, tm=tm,
                     tn=nq // 2, row_cols=[0], row_width=qr, extras=[(cos_t, tab), (sin_t, tab)],
                     name="mla_q")
    kv = fused_mm([lat], [ap["kv_norm"]], ap["w_kv"], _norm_prologue, _store_epilogue,
                  out_dtype=BF16, tm=tm, tn=ap["w_kv"].shape[1] // 2, row_cols=[1], row_width=qr,
                  name="mla_kv")
    krope = rope_k(lat, (n_lat - LANES) // LANES, cos_t, sin_t, tm=tm)
    a_out = mla_attention(q_all, kv, krope, B, S, H, tq=MXU_DIM, n_chain=2,
                          tk=_tile(S // 2, 1024))
    b_out = dilated_attention(qkv_b, bias_t, B, S, H, t=bias_t.shape[-1])
    return mm_post(a_out, b_out, ap["w_out"], g_post, x, tm=_tile(T, 256), name="attn_out")


def rwkv_layer(x, B, S, g_pre, g_post, rp):
    T, D = x.shape
    tm = _tile(S, 512)
    rkv = fused_mm([x], [g_pre, rp["mu_rkv"]], rp["w_rkv"], _shift_prologue, _store_epilogue,
                   out_dtype=F32, tm=tm, tn=D // 2, n_mix=3, tiles_per_mix=2, seq_len=S,
                   name="rwkv_rkv")
    l1 = fused_mm([x], [g_pre, rp["mu_l1"]], rp["w_l1"], _shift_prologue, _lora1_epilogue,
                  out_dtype=BF16, tm=tm, tn=MXU_DIM, n_mix=3, tiles_per_mix=1, seq_len=S,
                  name="rwkv_lora1")
    l2 = lora2(l1, rp["w_l2"], tm=_tile(T, 512), tn=D // 2 if D >= 512 else D, kb=MXU_DIM)
    y = rwkv_scan(rkv, l2, rp["w0"], rp["a0"], rp["k_k"], rp["k_a"], B, S, D)
    return rwkv_out(y, rkv, l2, rp["r_k"], rp["ln_w"], rp["ln_b"], rp["w_o"], g_post, x,
                    tm=_tile(T, 256))


def trunk(x3, p):
    B, S, D = x3.shape
    x = x3.reshape(B * S, D)
    depth = p["norm_g"].shape[0]
    for layer in range(depth):
        i = layer // 2
        g = p["norm_g"][layer]
        if layer % 2 == 0:
            x = attention_layer(x, B, S, g[0][None], g[1][None], p["attn"][i], p["bias_t"],
                                p["cos_t"][(B, S)], p["sin_t"][(B, S)], p["heads"])
        else:
            x = rwkv_layer(x, B, S, g[0][None], g[1][None], p["rwkv"][i])
        x = ffn(x, g[2][None], g[3][None], p["ffn_wg"][layer], p["ffn_wu"][layer],
                p["ffn_wd"][layer], tm=_tile(B * S, 512), tf=_tile(p["ffn_wg"][layer].shape[1], 512))
    return x.reshape(B, S, D)


def kernel(x_prompt, x_sample, norm_g, rel_bias, at_w_in, at_q_norm, at_kv_norm, at_w_uq, at_w_ukv, at_w_out, rw_mu, rw_w_r, rw_w_k, rw_w_v, rw_w_o, rw_w0, rw_w1, rw_w2, rw_a0, rw_a1, rw_a2, rw_g1, rw_g2, rw_k_k, rw_k_a, rw_r_k, rw_ln_w, rw_ln_b, ffn_w_gate, ffn_w_up, ffn_w_down):
    H = rel_bias.shape[1]
    p = {"norm_g": norm_g, "heads": H}
    p["attn"] = [_attn_params(at_w_in[i], at_q_norm[i], at_kv_norm[i], at_w_uq[i], at_w_ukv[i],
                              at_w_out[i], H) for i in range(at_w_in.shape[0])]
    p["rwkv"] = [_rwkv_params(rw_mu[i], rw_w_r[i], rw_w_k[i], rw_w_v[i], rw_w_o[i], rw_w0[i],
                              rw_w1[i], rw_w2[i], rw_a0[i], rw_a1[i], rw_a2[i], rw_g1[i],
                              rw_g2[i], rw_k_k[i], rw_k_a[i], rw_r_k[i], rw_ln_w[i], rw_ln_b[i])
                 for i in range(rw_mu.shape[0])]
    p["ffn_wg"] = ffn_w_gate.astype(BF16)
    p["ffn_wu"] = ffn_w_up.astype(BF16)
    p["ffn_wd"] = ffn_w_down.astype(BF16)
    p["bias_t"] = dilated_bias_tiles(rel_bias, MXU_DIM)
    p["cos_t"], p["sin_t"] = {}, {}
    for xs in (x_prompt, x_sample):
        B, S = xs.shape[:2]
        p["cos_t"][(B, S)], p["sin_t"][(B, S)] = _rope_tables(B, S)
    return trunk(x_prompt, p), trunk(x_sample, p)
```

```python
import functools
import math

import jax
import jax.numpy as jnp
import numpy as np
from jax import lax
from jax.experimental import pallas as pl
from jax.experimental.pallas import tpu as pltpu

F32 = jnp.float32
BF16 = jnp.bfloat16

MLA_NOPE_DIM = 128
MLA_ROPE_DIM = 64
MLA_V_DIM = 128
ROPE_BASE = 10000.0
DIL_HEAD_DIM = 128
DIL_PATTERNS = ((128, 1), (512, 4), (2048, 16))
N_BUCKETS = 32
T5_MAX_DISTANCE = 1024
RWKV_HEAD = 64
RWKV_LN_EPS = 64e-5
NORM_EPS = 1e-6
NEG_BIG = -1e30
LOG2E = 1.4426950408889634

LANES = 128
SUBLANES = 8
MXU_DIM = 256
VMEM_LIMIT = 56 * 1024 * 1024

SCAN_CHUNK = 64
SCAN_LANES = 2048
SCAN_GROUP = 128


def _cparams(sem, vmem=VMEM_LIMIT):
    return pltpu.CompilerParams(dimension_semantics=sem, vmem_limit_bytes=vmem)


def _rms(x, g):
    return x * lax.rsqrt(jnp.mean(x * x, axis=-1, keepdims=True) + NORM_EPS) * g


def _dot(a, b):
    return jnp.dot(a, b, preferred_element_type=F32)


def _dot_nt(a, b):
    return lax.dot_general(a, b, (((1,), (1,)), ((), ())), preferred_element_type=F32)


def _dot_tn(a, b):
    return lax.dot_general(a, b, (((0,), (0,)), ((), ())), preferred_element_type=F32)


def _sigmoid(x):
    return 1.0 / (1.0 + jnp.exp(-x))


def _fused_mm_kernel(*refs, n_rows, n_vecs, n_extra, n_mix, tiles_per_mix, tiles_per_seq,
                     prologue, epilogue):
    rows = refs[:n_rows]
    vecs = refs[n_rows:n_rows + n_vecs]
    w_ref = refs[n_rows + n_vecs]
    base = n_rows + n_vecs + 1
    extras = refs[base:base + n_extra]
    o_ref = refs[base + n_extra]
    h_ref = refs[base + n_extra + 1]
    j = pl.program_id(1)
    flags = ()
    if tiles_per_seq:
        it = pl.program_id(0) % tiles_per_seq
        flags = (jnp.where(it == 0, 0.0, 1.0), jnp.where(it == tiles_per_seq - 1, 0.0, 1.0))

    @pl.when(j == 0)
    def _():
        hs = prologue(*[r[...] for r in rows], *flags, *[v[...] for v in vecs])
        for m in range(n_mix):
            h_ref[m] = hs[m].astype(BF16)

    if n_mix == 1:
        h = h_ref[0]
    else:
        h = h_ref[j // tiles_per_mix]
    acc = _dot(h, w_ref[...])
    epilogue(j, acc, o_ref, *extras)


def fused_mm(rows, vecs, w, prologue, epilogue, *, out_dtype, tm, tn, row_cols=None,
             row_width=None, n_mix=1, tiles_per_mix=1, extras=(), seq_len=None, name):
    T = rows[0].shape[0]
    K, N = w.shape
    row_cols = row_cols or [0] * len(rows)
    row_width = row_width or K
    assert T % tm == 0 and N % tn == 0, (T, tm, N, tn)
    in_specs = [pl.BlockSpec((tm, row_width), functools.partial(lambda i, j, c: (i, c), c=c))
                for c in row_cols]
    if seq_len:
        assert len(rows) == 1 and seq_len % tm == 0
        g8, last8 = tm // SUBLANES, T // SUBLANES - 1
        rows = [rows[0]] * 3
        in_specs += [pl.BlockSpec((SUBLANES, K), lambda i, j: (jnp.maximum(i * g8 - 1, 0), 0)),
                     pl.BlockSpec((SUBLANES, K), lambda i, j: (jnp.minimum((i + 1) * g8, last8), 0))]
    in_specs += [pl.BlockSpec(v.shape, lambda i, j: (0, 0)) for v in vecs]
    in_specs += [pl.BlockSpec((K, tn), lambda i, j: (0, j))]
    in_specs += [spec for _, spec in extras]
    kern = functools.partial(
        _fused_mm_kernel, n_rows=len(rows), n_vecs=len(vecs), n_extra=len(extras), n_mix=n_mix,
        tiles_per_mix=tiles_per_mix, tiles_per_seq=seq_len // tm if seq_len else 0,
        prologue=prologue, epilogue=epilogue)
    return pl.pallas_call(
        kern,
        grid=(T // tm, N // tn),
        in_specs=in_specs,
        out_specs=pl.BlockSpec((tm, tn), lambda i, j: (i, j)),
        out_shape=jax.ShapeDtypeStruct((T, N), out_dtype),
        scratch_shapes=[pltpu.VMEM((n_mix, tm, K), BF16)],
        compiler_params=_cparams(("parallel", "arbitrary")),
        name=name,
    )(*rows, *vecs, w, *[a for a, _ in extras])


def _norm_prologue(x, g):
    return (_rms(x, g),)


def _store_epilogue(j, acc, o_ref):
    o_ref[...] = acc.astype(o_ref.dtype)


def _scale_epilogue(j, acc, o_ref, s_ref):
    o_ref[...] = (acc * s_ref[...]).astype(o_ref.dtype)


def _mm_post_kernel(a_ref, b_ref, w_ref, g_ref, x_ref, o_ref):
    lhs = jnp.concatenate([a_ref[...], b_ref[...]], axis=-1)
    y = _dot(lhs, w_ref[...])
    o_ref[...] = x_ref[...] + _rms(y, g_ref[...])


def mm_post(a, b, w, g, x, *, tm, name):
    T, Ka = a.shape
    Kb = b.shape[1]
    D = w.shape[1]
    return pl.pallas_call(
        _mm_post_kernel,
        grid=(T // tm,),
        in_specs=[pl.BlockSpec((tm, Ka), lambda i: (i, 0)),
                  pl.BlockSpec((tm, Kb), lambda i: (i, 0)),
                  pl.BlockSpec((Ka + Kb, D), lambda i: (0, 0), pipeline_mode=pl.Buffered(1)),
                  pl.BlockSpec((1, D), lambda i: (0, 0)),
                  pl.BlockSpec((tm, D), lambda i: (i, 0))],
        out_specs=pl.BlockSpec((tm, D), lambda i: (i, 0)),
        out_shape=jax.ShapeDtypeStruct((T, D), F32),
        compiler_params=_cparams(("parallel",)),
        name=name,
    )(a, b, w, g, x)


def _ffn_kernel(x_ref, gpre_ref, gpost_ref, wg_ref, wu_ref, wd_ref, o_ref, h_ref, acc_ref):
    f = pl.program_id(1)

    @pl.when(f == 0)
    def _():
        h_ref[...] = _rms(x_ref[...], gpre_ref[...]).astype(BF16)
        acc_ref[...] = jnp.zeros_like(acc_ref)

    h = h_ref[...]
    gate = _dot(h, wg_ref[...])
    up = _dot(h, wu_ref[...])
    act = (gate * _sigmoid(gate) * up).astype(BF16)
    acc_ref[...] += _dot(act, wd_ref[...])

    @pl.when(f == pl.num_programs(1) - 1)
    def _():
        o_ref[...] = x_ref[...] + _rms(acc_ref[...], gpost_ref[...])


def ffn(x, g_pre, g_post, wg, wu, wd, *, tm, tf):
    T, D = x.shape
    Fh = wg.shape[1]
    assert T % tm == 0 and Fh % tf == 0
    return pl.pallas_call(
        _ffn_kernel,
        grid=(T // tm, Fh // tf),
        in_specs=[pl.BlockSpec((tm, D), lambda i, f: (i, 0)),
                  pl.BlockSpec((1, D), lambda i, f: (0, 0)),
                  pl.BlockSpec((1, D), lambda i, f: (0, 0)),
                  pl.BlockSpec((D, tf), lambda i, f: (0, f)),
                  pl.BlockSpec((D, tf), lambda i, f: (0, f)),
                  pl.BlockSpec((tf, D), lambda i, f: (f, 0))],
        out_specs=pl.BlockSpec((tm, D), lambda i, f: (i, 0)),
        out_shape=jax.ShapeDtypeStruct((T, D), F32),
        scratch_shapes=[pltpu.VMEM((tm, D), BF16), pltpu.VMEM((tm, D), F32)],
        compiler_params=_cparams(("parallel", "arbitrary")),
        name="ffn",
    )(x, g_pre, g_post, wg, wu, wd)


def _rope_groups(x, cos_t, sin_t):
    half = MLA_ROPE_DIM // 2
    n = x.shape[-1] // LANES
    lane = lax.broadcasted_iota(jnp.int32, x.shape, 1) % LANES
    partner = jnp.where(lane < half, pltpu.roll(x, x.shape[-1] - half, 1), pltpu.roll(x, half, 1))
    if n > 1:
        cos_t = jnp.concatenate([cos_t] * n, axis=-1)
        sin_t = jnp.concatenate([sin_t] * n, axis=-1)
    return x * cos_t + partner * sin_t


def _q_epilogue(j, acc, o_ref, cos_ref, sin_ref, *, scale):
    @pl.when(j == 0)
    def _():
        o_ref[...] = (acc * scale).astype(o_ref.dtype)

    @pl.when(j == 1)
    def _():
        o_ref[...] = (_rope_groups(acc, cos_ref[...], sin_ref[...]) * scale).astype(o_ref.dtype)


def _rope_k_kernel(x_ref, cos_ref, sin_ref, o_ref):
    o_ref[...] = _rope_groups(x_ref[...], cos_ref[...], sin_ref[...]).astype(o_ref.dtype)


def rope_k(lat, col_block, cos_t, sin_t, *, tm):
    T = lat.shape[0]
    return pl.pallas_call(
        _rope_k_kernel,
        grid=(T // tm,),
        in_specs=[pl.BlockSpec((tm, LANES), lambda i: (i, col_block)),
                  pl.BlockSpec((tm, LANES), lambda i: (i, 0)),
                  pl.BlockSpec((tm, LANES), lambda i: (i, 0))],
        out_specs=pl.BlockSpec((tm, LANES), lambda i: (i, 0)),
        out_shape=jax.ShapeDtypeStruct((T, LANES), BF16),
        compiler_params=_cparams(("parallel",)),
        name="rope_k",
    )(lat, cos_t, sin_t)


def _softmax_weights(ss, m_prev):
    m_new, ps = [], []
    for h, tiles in enumerate(ss):
        m = m_prev[h]
        for s in tiles:
            m = jnp.maximum(m, jnp.max(s, axis=0, keepdims=True))
        m_new.append(m)
    for h, tiles in enumerate(ss):
        ps.append([jnp.exp2(s - m_new[h]) for s in tiles])
    return m_new, ps


def _weighted_values(ps, vs, l, acc):
    for p, v in zip(ps, vs):
        l = l + jnp.sum(p, axis=0, keepdims=True)
        acc = acc + _dot_tn(v, p.astype(BF16))
    return l, acc


def _mla_kernel(qn_ref, qr_ref, kn_ref, kr_ref, v_ref, o_ref, m_ref, l_ref, acc_ref, sa_ref,
                sb_ref, *, tk):
    nkv = kn_ref.shape[0] // tk
    assert nkv % 2 == 0
    n, _, tq = acc_ref.shape
    q = jnp.concatenate([qn_ref[...], qr_ref[...]], axis=-1)
    qs = [q[h * tq:(h + 1) * tq] for h in range(n)]
    m_ref[...] = jnp.full_like(m_ref, NEG_BIG)
    l_ref[...] = jnp.zeros_like(l_ref)
    acc_ref[...] = jnp.zeros_like(acc_ref)

    def chunk(c):
        return pl.ds(c * tk if isinstance(c, int) else pl.multiple_of(c * tk, tk), tk)

    def put_scores(c, s_ref):
        k = jnp.concatenate([kn_ref[chunk(c), :], kr_ref[chunk(c), :]], axis=-1)
        for h in range(n):
            s_ref[h] = _dot_nt(k, qs[h])

    def consume(c, s_ref):
        v = v_ref[chunk(c), :]
        m_prev = [m_ref[h] for h in range(n)]
        m_new, ps = _softmax_weights([[s_ref[h]] for h in range(n)], m_prev)
        for h in range(n):
            alpha = jnp.exp2(m_prev[h] - m_new[h])
            l, acc = _weighted_values(ps[h], [v], alpha * l_ref[h], alpha * acc_ref[h])
            l_ref[h] = l
            acc_ref[h] = acc
            m_ref[h] = m_new[h]

    put_scores(0, sa_ref)

    def body(i, carry):
        c = 2 * i
        put_scores(c + 1, sb_ref)
        consume(c, sa_ref)
        put_scores(c + 2, sa_ref)
        consume(c + 1, sb_ref)
        return carry

    lax.fori_loop(0, nkv // 2 - 1, body, 0)
    put_scores(nkv - 1, sb_ref)
    consume(nkv - 2, sa_ref)
    consume(nkv - 1, sb_ref)
    for h in range(n):
        o_ref[h * tq:(h + 1) * tq, :] = (acc_ref[h] / l_ref[h]).T.astype(o_ref.dtype)


def mla_attention(q_all, kv, krope, B, S, H, *, tq, n_chain, tk):
    T = B * S
    tb = tq * n_chain
    nq = S // tb
    dv = MLA_V_DIM
    return pl.pallas_call(
        functools.partial(_mla_kernel, tk=tk),
        grid=(B, H, nq),
        in_specs=[pl.BlockSpec((tb, LANES), lambda b, h, i: (b * nq + i, h)),
                  pl.BlockSpec((tb, LANES), lambda b, h, i: (b * nq + i, H + h)),
                  pl.BlockSpec((S, LANES), lambda b, h, i: (b, h)),
                  pl.BlockSpec((S, LANES), lambda b, h, i: (b, 0)),
                  pl.BlockSpec((S, LANES), lambda b, h, i: (b, H + h))],
        out_specs=pl.BlockSpec((tb, dv), lambda b, h, i: (b * nq + i, h)),
        out_shape=jax.ShapeDtypeStruct((T, H * dv), BF16),
        scratch_shapes=[pltpu.VMEM((n_chain, 1, tq), F32), pltpu.VMEM((n_chain, 1, tq), F32),
                        pltpu.VMEM((n_chain, dv, tq), F32), pltpu.VMEM((n_chain, tk, tq), F32),
                        pltpu.VMEM((n_chain, tk, tq), F32)],
        compiler_params=_cparams(("parallel", "parallel", "arbitrary")),
        name="mla_attention",
    )(q_all, q_all, kv, krope, kv)


def _dil_kernel(q_ref, k_ref, v_ref, bias_ref, o_ref, *, t, nside):
    nk = k_ref.shape[0] // t
    i = pl.program_id(2)
    q = q_ref[...]
    ss, vs = [], []
    for d in range(2 * nside + 1):
        c = i + d - nside
        inside = jnp.logical_and(c >= 0, c < nk)
        rows = pl.ds(pl.multiple_of(jnp.clip(c, 0, nk - 1) * t, t), t)
        ss.append(_dot_nt(k_ref[rows, :], q) + (bias_ref[d] + jnp.where(inside, 0.0, NEG_BIG)))
        vs.append(v_ref[rows, :])
    m_new, ps = _softmax_weights([ss], [jnp.full((1, t), NEG_BIG, F32)])
    l, acc = _weighted_values(ps[0], vs, jnp.zeros((1, t), F32), jnp.zeros((DIL_HEAD_DIM, t), F32))
    o_ref[...] = (acc / l).T.astype(o_ref.dtype)


def dilated_attention(qkv, bias_t, B, S, H, *, t):
    T = B * S
    nq = S // t
    nd = bias_t.shape[0]
    dh = DIL_HEAD_DIM
    return pl.pallas_call(
        functools.partial(_dil_kernel, t=t, nside=(nd - 1) // 2),
        grid=(B, H, nq),
        in_specs=[pl.BlockSpec((t, dh), lambda b, h, i: (b * nq + i, h)),
                  pl.BlockSpec((S, dh), lambda b, h, i: (b, H + h)),
                  pl.BlockSpec((S, dh), lambda b, h, i: (b, 2 * H + h)),
                  pl.BlockSpec((nd, None, t, t), lambda b, h, i: (0, h, 0, 0))],
        out_specs=pl.BlockSpec((t, dh), lambda b, h, i: (b * nq + i, h)),
        out_shape=jax.ShapeDtypeStruct((T, H * dh), BF16),
        compiler_params=_cparams(("parallel", "parallel", "arbitrary")),
        name="dilated_attention",
    )(qkv, qkv, qkv, bias_t)


def _t5_bucket_np(rel):
    half = N_BUCKETS // 2
    max_exact = half // 2
    bucket = np.where(rel > 0, half, 0)
    n = np.abs(rel)
    nf = np.maximum(n, 1).astype(np.float64)
    large = max_exact + (np.log(nf / max_exact) / math.log(T5_MAX_DISTANCE / max_exact)
                         * (half - max_exact)).astype(np.int64)
    large = np.minimum(large, half - 1)
    return bucket + np.where(n < max_exact, n, large)


def dilated_bias_tiles(rel_bias, t):
    reach = max(w // 2 for w, _ in DIL_PATTERNS)
    nside = -(-reach // t)
    nd = 2 * nside + 1
    span = (nside + 1) * t
    delta = np.arange(-span + 1, span)
    mult = np.zeros(delta.shape, np.int64)
    for window, dil in DIL_PATTERNS:
        mult += ((delta % dil == 0) & (np.abs(delta) <= window // 2)).astype(np.int64)
    logm = np.log(np.maximum(mult, 1)).astype(np.float32)
    f = (rel_bias.astype(F32)[_t5_bucket_np(delta)] + logm[:, None]) * LOG2E
    f = jnp.where((mult > 0)[:, None], f, NEG_BIG)
    H = f.shape[1]
    frev = jnp.pad(f[::-1].T, ((0, 0), (1, 0)))
    wins = []
    for d in range(nd):
        s_d = span - 1 - (d - nside) * t + 1
        wins.append(jnp.concatenate([frev[:, s_d:s_d + t], frev[:, s_d - t:s_d]], axis=1))
    g = jnp.stack(wins, axis=0)
    flat = jnp.tile(g, (1, 1, t))[:, :, :t * (2 * t - 1)]
    return flat.reshape(nd, H, t, 2 * t - 1)[:, :, :, :t]


def _shift_prologue(x, x_before, x_after, keep_before, keep_after, g, mu):
    tm = x.shape[0]
    h = _rms(x, g)
    edge_prev = _rms(x_before[SUBLANES - 1:SUBLANES, :], g) * keep_before
    edge_next = _rms(x_after[0:1, :], g) * keep_after
    row = lax.broadcasted_iota(jnp.int32, h.shape, 0)
    h_prev = jnp.where(row == 0, edge_prev, pltpu.roll(h, 1, 0))
    h_next = jnp.where(row == tm - 1, edge_next, pltpu.roll(h, tm - 1, 0))
    xx = 0.5 * (h_prev + h_next) - h
    return tuple(h + xx * mu[m:m + 1, :] for m in range(mu.shape[0]))


def _lora1_epilogue(j, acc, o_ref):
    @pl.when(j == 0)
    def _():
        o_ref[...] = _sigmoid(acc).astype(o_ref.dtype)

    @pl.when(j == 1)
    def _():
        o_ref[...] = jnp.tanh(acc).astype(o_ref.dtype)

    @pl.when(j == 2)
    def _():
        o_ref[...] = acc.astype(o_ref.dtype)


def _lora2_kernel(a_ref, w_ref, b_ref, o_ref, *, tiles_per_d):
    j = pl.program_id(1)
    z = _dot(a_ref[...], w_ref[...]) + b_ref[...]

    @pl.when(j < tiles_per_d)
    def _():
        o_ref[...] = z

    @pl.when(jnp.logical_and(j >= tiles_per_d, j < 3 * tiles_per_d))
    def _():
        w_log = -(jnp.maximum(-z, 0.0) + jnp.log(1.0 + jnp.exp(-jnp.abs(z)))) - 0.5
        o_ref[...] = -jnp.exp(w_log)

    @pl.when(j >= 3 * tiles_per_d)
    def _():
        o_ref[...] = _sigmoid(z)


def lora2(l1, w, bias, *, tm, tn, kb):
    T = l1.shape[0]
    N = w.shape[1]
    D = N // 5
    assert D % tn == 0

    def a_map(i, j):
        return (i, (j * tn + D) // (2 * D))

    return pl.pallas_call(
        functools.partial(_lora2_kernel, tiles_per_d=D // tn),
        grid=(T // tm, N // tn),
        in_specs=[pl.BlockSpec((tm, kb), a_map), pl.BlockSpec((kb, tn), lambda i, j: (0, j)),
                  pl.BlockSpec((1, tn), lambda i, j: (0, j))],
        out_specs=pl.BlockSpec((tm, tn), lambda i, j: (i, j)),
        out_shape=jax.ShapeDtypeStruct((T, N), F32),
        compiler_params=_cparams(("parallel", "parallel")),
        name="rwkv_lora2",
    )(l1, w, bias)


def _head_block_mask(n):
    r = lax.broadcasted_iota(jnp.int32, (n, n), 0) // RWKV_HEAD
    c = lax.broadcasted_iota(jnp.int32, (n, n), 1) // RWKV_HEAD
    return r == c


def _head_sums(x):
    ones = jnp.where(_head_block_mask(MXU_DIM), 1.0, 0.0).astype(BF16)
    parts = [_dot(x[:, g:g + MXU_DIM].astype(BF16), ones) for g in range(0, x.shape[-1], MXU_DIM)]
    return parts[0] if len(parts) == 1 else jnp.concatenate(parts, axis=-1)


def _split3(x):
    hi = x.astype(BF16)
    r1 = x - hi.astype(F32)
    mid = r1.astype(BF16)
    lo = (r1 - mid.astype(F32)).astype(BF16)
    return hi, mid, lo


def _scan_kernel(r_ref, k_ref, v_ref, lw_ref, a_ref, kk_ref, ka_ref, y_ref, s_ref):
    C = SCAN_CHUNK
    HN = RWKV_HEAD
    d = pl.program_id(1)
    c = pl.program_id(3)
    fwd = d == 0

    @pl.when(c == 0)
    def _():
        s_ref[...] = jnp.zeros_like(s_ref)

    r = r_ref[...]
    k = k_ref[...]
    v = v_ref[...]
    kk = k * kk_ref[...]
    kk = kk * lax.rsqrt(_head_sums(kk * kk) + 1e-12)
    lw = lw_ref[...]
    a = a_ref[...]
    kd = k * (1.0 + (a - 1.0) * ka_ref[...])
    aa = -kk
    bb = kk * a

    sign = 1 - 2 * d
    trow = lax.broadcasted_iota(jnp.int32, (C, C), 0)
    tcol = lax.broadcasted_iota(jnp.int32, (C, C), 1)
    cum = jnp.where((trow - tcol) * sign >= 0, 1.0, 0.0).astype(BF16)
    hi, mid, lo = _split3(lw)
    l_inc = _dot(cum, hi) + _dot(cum, mid) + _dot(cum, lo)
    l_exc = l_inc - lw
    l_ref = l_inc[C // 2:C // 2 + 1, :]
    l_end = jnp.where(fwd, l_inc[C - 1:C, :], l_inc[0:1, :])
    e_in = jnp.exp(l_inc - l_ref)
    e_ex = jnp.exp(l_exc - l_ref)
    e_ng = jnp.exp(l_ref - l_inc)
    rho = jnp.exp(l_ref)
    p_end = jnp.exp(l_end)
    end_over_ref = jnp.exp(l_end - l_ref)
    at = aa * e_ex
    rt = r * e_in
    bt = bb * e_ng
    kt = kd * e_ng
    at_s = at * rho
    rt_s = rt * rho
    bh = bt * end_over_ref
    kh = kt * end_over_ref

    GW = min(SCAN_GROUP, r.shape[-1])
    cat = jnp.concatenate
    lane_head = lax.broadcasted_iota(jnp.int32, (1, GW), 1) // HN

    def mstack(x):
        zero = jnp.zeros_like(x)
        return cat([jnp.where(lane_head == j, x, zero) for j in range(GW // HN)], axis=0)

    trow = lax.broadcasted_iota(jnp.int32, (C, GW), 0)
    scol = lax.broadcasted_iota(jnp.int32, (C, GW), 1) % HN
    strict = (trow - scol) * sign > 0
    incl = (trow - scol) * sign >= 0
    eye = jnp.where(scol == trow, 1.0, 0.0)
    bdiag = _head_block_mask(GW)

    groups = [slice(p * GW, (p + 1) * GW) for p in range(r.shape[-1] // GW)]
    v_b = [v[:, sl].astype(BF16) for sl in groups]
    sc = [_dot_nt(cat([at[:, sl], rt[:, sl]], axis=0).astype(BF16),
                  cat([mstack(bt[:, sl].astype(BF16)), mstack(kt[:, sl].astype(BF16))], axis=0))
          for sl in groups]
    a_ab = [jnp.where(strict, s[:C, :GW], 0.0) for s in sc]
    a_ak = [jnp.where(strict, s[:C, GW:], 0.0) for s in sc]
    a_r = [cat([jnp.where(incl, s[C:, :GW], 0.0), jnp.where(incl, s[C:, GW:], 0.0)],
               axis=1).astype(BF16) for s in sc]

    av = [_dot(x.astype(BF16), mstack(vb)) for x, vb in zip(a_ak, v_b)]
    tinv = None
    for lvl in range(int(math.log2(C))):
        couples = jnp.logical_and(((trow >> lvl) & 1) - ((scol >> lvl) & 1) == sign,
                                  (trow >> (lvl + 1)) == (scol >> (lvl + 1)))
        low = [jnp.where(couples, x, 0.0) for x in a_ab]
        if tinv is None:
            tinv = [eye + x for x in low]
            continue
        ld = [_dot(x.astype(BF16), mstack(t.astype(BF16))) for x, t in zip(low, tinv)]
        tinv = [t + _dot(t.astype(BF16), mstack(x.astype(BF16))) for t, x in zip(tinv, ld)]

    uw = [_dot(t.astype(BF16),
               cat([mstack(x.astype(BF16)), mstack(at_s[:, sl].astype(BF16))], axis=1))
          for t, x, sl in zip(tinv, av, groups)]

    s_old = [s_ref[p] for p in range(len(groups))]
    us = [_dot_nt(cat([w[:, GW:], rt_s[:, sl]], axis=0).astype(BF16), s.astype(BF16))
          for w, sl, s in zip(uw, groups, s_old)]
    u = [x[:C] + w[:, :GW] for x, w in zip(us, uw)]
    y = [x[C:] + _dot(ar, cat([mstack(uu.astype(BF16)), mstack(vb)], axis=0))
         for x, ar, uu, vb in zip(us, a_r, u, v_b)]
    for sl, yy in zip(groups, y):
        y_ref[:, sl] = yy

    s_new = [_dot(cat([uu, v[:, sl]], axis=0).T.astype(BF16),
                  cat([bh[:, sl], kh[:, sl]], axis=0).astype(BF16))
             for uu, sl in zip(u, groups)]
    for p, sl in enumerate(groups):
        s_ref[p] = jnp.where(bdiag, s_new[p], 0.0) + s_old[p] * p_end[:, sl]


def rwkv_scan(rkv, l2, k_k, k_a, B, S, D):
    T = B * S
    C = SCAN_CHUNK
    L = min(SCAN_LANES, D)
    gw = min(SCAN_GROUP, L)
    nc = S // C
    ng = D // L

    def row(b, d, c):
        return b * nc + c + d * (nc - 1 - 2 * c)

    def rspec(off):
        return pl.BlockSpec((C, L), lambda b, d, g, c: (row(b, d, c), off * ng + g))

    def dspec(off):
        return pl.BlockSpec((C, L), lambda b, d, g, c: (row(b, d, c), (off + d) * ng + g))

    vec = pl.BlockSpec((1, L), lambda b, d, g, c: (0, g))
    return pl.pallas_call(
        _scan_kernel,
        grid=(B, 2, ng, nc),
        in_specs=[rspec(0), rspec(1), rspec(2), dspec(1), dspec(3), vec, vec],
        out_specs=pl.BlockSpec((None, C, L), lambda b, d, g, c: (d, row(b, d, c), g)),
        out_shape=jax.ShapeDtypeStruct((2, T, D), F32),
        scratch_shapes=[pltpu.VMEM((L // gw, gw, gw), F32)],
        compiler_params=_cparams(("parallel", "parallel", "parallel", "arbitrary")),
        name="rwkv_scan",
    )(rkv, rkv, rkv, l2, l2, k_k, k_a)


def _rwkv_out_kernel(y_ref, r_ref, k_ref, v_ref, g_ref, rk_ref, lnw_ref, lnb_ref, wo_ref,
                     gpost_ref, x_ref, o_ref):
    inv_n = 1.0 / RWKV_HEAD
    y = y_ref[0] + y_ref[1]
    mean = _head_sums(y) * inv_n
    yc = y - mean
    var = _head_sums(yc * yc) * inv_n
    yn = yc * lax.rsqrt(var + RWKV_LN_EPS) * lnw_ref[...] + lnb_ref[...]
    bonus = _head_sums(r_ref[...] * k_ref[...] * rk_ref[...]) * v_ref[...]
    mix = ((yn + bonus) * g_ref[...]).astype(BF16)
    out = _dot(mix, wo_ref[...])
    o_ref[...] = x_ref[...] + _rms(out, gpost_ref[...])


def rwkv_out(y, rkv, l2, r_k, ln_w, ln_b, w_o, g_post, x, *, tm):
    T, D = x.shape
    vec = pl.BlockSpec((1, D), lambda i: (0, 0))
    return pl.pallas_call(
        _rwkv_out_kernel,
        grid=(T // tm,),
        in_specs=[pl.BlockSpec((2, tm, D), lambda i: (0, i, 0)),
                  pl.BlockSpec((tm, D), lambda i: (i, 0)),
                  pl.BlockSpec((tm, D), lambda i: (i, 1)),
                  pl.BlockSpec((tm, D), lambda i: (i, 2)),
                  pl.BlockSpec((tm, D), lambda i: (i, 0)),
                  vec, vec, vec,
                  pl.BlockSpec((D, D), lambda i: (0, 0), pipeline_mode=pl.Buffered(1)),
                  vec,
                  pl.BlockSpec((tm, D), lambda i: (i, 0))],
        out_specs=pl.BlockSpec((tm, D), lambda i: (i, 0)),
        out_shape=jax.ShapeDtypeStruct((T, D), F32),
        compiler_params=_cparams(("parallel",)),
        name="rwkv_out",
    )(y, rkv, rkv, rkv, l2, r_k, ln_w, ln_b, w_o, g_post, x)


def _pad_cols(w, n):
    return jnp.pad(w, ((0, 0), (0, n - w.shape[1])))


def _pad_rows(w, n):
    return jnp.pad(w, ((0, n - w.shape[0]), (0, 0)))


def _attn_params(w_in, q_norm, kv_norm, w_uq, w_ukv, w_out, H):
    qr, kvr = q_norm.shape[0], kv_norm.shape[0]
    rope, nope, vd = MLA_ROPE_DIM, MLA_NOPE_DIM, MLA_V_DIM
    w_lat = jnp.concatenate([w_in[:, :qr + kvr], _pad_cols(w_in[:, qr + kvr:qr + kvr + rope], LANES)],
                            axis=1).astype(BF16)
    w_dil = w_in[:, qr + kvr + rope:].astype(BF16)
    uq = w_uq.reshape(qr, H, nope + rope)
    uq_rope = jnp.pad(uq[:, :, nope:], ((0, 0), (0, 0), (0, LANES - rope)))
    w_q = jnp.concatenate([uq[:, :, :nope].reshape(qr, H * nope), uq_rope.reshape(qr, H * LANES)],
                          axis=1).astype(BF16)
    ukv = w_ukv.reshape(kvr, H, nope + vd)
    w_kv = jnp.concatenate([ukv[:, :, :nope].reshape(kvr, H * nope),
                            ukv[:, :, nope:].reshape(kvr, H * vd)], axis=1).astype(BF16)
    return dict(w_lat=w_lat, w_dil=w_dil, w_q=w_q, w_kv=w_kv, w_out=w_out.astype(BF16),
                q_norm=q_norm[None], kv_norm=kv_norm[None])


def _rope_tables(B, S):
    half = MLA_ROPE_DIM // 2
    inv = ROPE_BASE ** (-jnp.arange(0, MLA_ROPE_DIM, 2, dtype=F32) / MLA_ROPE_DIM)
    ang = jnp.arange(S, dtype=F32)[:, None] * inv[None, :]
    cos, sin = jnp.cos(ang), jnp.sin(ang)
    zeros = jnp.zeros((S, LANES - 2 * half), F32)
    cos_t = jnp.concatenate([cos, cos, zeros], axis=1)
    sin_t = jnp.concatenate([-sin, sin, zeros], axis=1)
    return jnp.tile(cos_t, (B, 1)), jnp.tile(sin_t, (B, 1))


def _rwkv_params(mu, w_r, w_k, w_v, w_o, w0, w1, w2, a0, a1, a2, g1, g2, k_k, k_a, r_k, ln_w,
                 ln_b):
    D = w_r.shape[0]
    kb = MXU_DIM
    w_rkv = jnp.concatenate([w_r, w_k, w_v], axis=1).astype(BF16)
    w_l1 = jnp.concatenate([_pad_cols(g1, kb), _pad_cols(jnp.concatenate([w1[0], w1[1]], 1), kb),
                            _pad_cols(jnp.concatenate([a1[0], a1[1]], 1), kb)], axis=1).astype(BF16)
    rw = w2.shape[1]
    ra = a2.shape[1]
    z = lambda n: jnp.zeros((n, D), F32)
    w_l2 = jnp.concatenate([
        _pad_rows(g2, kb),
        _pad_rows(w2[0], kb),
        _pad_rows(jnp.concatenate([z(rw), w2[1]], 0), kb),
        _pad_rows(a2[0], kb),
        _pad_rows(jnp.concatenate([z(ra), a2[1]], 0), kb)], axis=1).astype(BF16)
    return dict(w_rkv=w_rkv, w_l1=w_l1, w_l2=w_l2, w_o=w_o.astype(BF16),
                mu_rkv=jnp.stack([mu[0], mu[2], mu[3]]), mu_l1=jnp.stack([mu[5], mu[1], mu[4]]),
                b_l2=jnp.concatenate([jnp.zeros((D,), F32), w0[0], w0[1], a0[0], a0[1]])[None],
                k_k=k_k[None], k_a=k_a[None],
                r_k=r_k.reshape(1, D), ln_w=ln_w[None], ln_b=ln_b[None])


def _tile(n, pref):
    t = min(pref, n)
    while n % t:
        t //= 2
    return t


def attention_layer(x, B, S, g_pre, g_post, ap, bias_t, cos_t, sin_t, H):
    T, D = x.shape
    tm = _tile(T, 512)
    qr = ap["q_norm"].shape[1]
    n_lat = ap["w_lat"].shape[1]
    lat = fused_mm([x], [g_pre], ap["w_lat"], _norm_prologue, _store_epilogue, out_dtype=F32,
                   tm=tm, tn=n_lat, name="attn_latent")
    n_dil = ap["w_dil"].shape[1]
    dil_scale = jnp.concatenate([jnp.full((1, n_dil // 3), DIL_HEAD_DIM ** -0.5 * LOG2E, F32),
                                 jnp.ones((1, 2 * n_dil // 3), F32)], axis=1)
    qkv_b = fused_mm([x], [g_pre], ap["w_dil"], _norm_prologue, _scale_epilogue, out_dtype=BF16,
                     tm=tm, tn=n_dil // 3,
                     extras=[(dil_scale, pl.BlockSpec((1, n_dil // 3), lambda i, j: (0, j)))],
                     name="attn_dil_qkv")
    q_scale = (MLA_NOPE_DIM + MLA_ROPE_DIM) ** -0.5 * LOG2E
    nq = ap["w_q"].shape[1]
    tab = pl.BlockSpec((tm, LANES), lambda i, j: (i, 0))
    q_all = fused_mm([lat], [ap["q_norm"]], ap["w_q"], _norm_prologue,
                     functools.partial(_q_epilogue, scale=q_scale), out_dtype=BF16, tm=tm,
                     tn=nq // 2, row_cols=[0], row_width=qr, extras=[(cos_t, tab), (sin_t, tab)],
                     name="mla_q")
    kv = fused_mm([lat], [ap["kv_norm"]], ap["w_kv"], _norm_prologue, _store_epilogue,
                  out_dtype=BF16, tm=tm, tn=ap["w_kv"].shape[1] // 2, row_cols=[1], row_width=qr,
                  name="mla_kv")
    krope = rope_k(lat, (n_lat - LANES) // LANES, cos_t, sin_t, tm=tm)
    a_out = mla_attention(q_all, kv, krope, B, S, H, tq=MXU_DIM, n_chain=4,
                          tk=_tile(S // 2, 1024))
    b_out = dilated_attention(qkv_b, bias_t, B, S, H, t=bias_t.shape[-1])
    return mm_post(a_out, b_out, ap["w_out"], g_post, x, tm=_tile(T, 256), name="attn_out")


def rwkv_layer(x, B, S, g_pre, g_post, rp):
    T, D = x.shape
    tm = _tile(S, 512)
    rkv = fused_mm([x], [g_pre, rp["mu_rkv"]], rp["w_rkv"], _shift_prologue, _store_epilogue,
                   out_dtype=F32, tm=tm, tn=D // 2, n_mix=3, tiles_per_mix=2, seq_len=S,
                   name="rwkv_rkv")
    l1 = fused_mm([x], [g_pre, rp["mu_l1"]], rp["w_l1"], _shift_prologue, _lora1_epilogue,
                  out_dtype=BF16, tm=tm, tn=MXU_DIM, n_mix=3, tiles_per_mix=1, seq_len=S,
                  name="rwkv_lora1")
    l2 = lora2(l1, rp["w_l2"], rp["b_l2"], tm=_tile(T, 512), tn=D // 2 if D >= 512 else D,
               kb=MXU_DIM)
    y = rwkv_scan(rkv, l2, rp["k_k"], rp["k_a"], B, S, D)
    return rwkv_out(y, rkv, l2, rp["r_k"], rp["ln_w"], rp["ln_b"], rp["w_o"], g_post, x,
                    tm=_tile(T, 256))


def trunk(x3, p):
    B, S, D = x3.shape
    x = x3.reshape(B * S, D)
    depth = p["norm_g"].shape[0]
    for layer in range(depth):
        i = layer // 2
        g = p["norm_g"][layer]
        if layer % 2 == 0:
            x = attention_layer(x, B, S, g[0][None], g[1][None], p["attn"][i], p["bias_t"],
                                p["cos_t"][(B, S)], p["sin_t"][(B, S)], p["heads"])
        else:
            x = rwkv_layer(x, B, S, g[0][None], g[1][None], p["rwkv"][i])
        x = ffn(x, g[2][None], g[3][None], p["ffn_wg"][layer], p["ffn_wu"][layer],
                p["ffn_wd"][layer], tm=_tile(B * S, 512), tf=_tile(p["ffn_wg"][layer].shape[1], 512))
    return x.reshape(B, S, D)


def kernel(x_prompt, x_sample, norm_g, rel_bias, at_w_in, at_q_norm, at_kv_norm, at_w_uq, at_w_ukv, at_w_out, rw_mu, rw_w_r, rw_w_k, rw_w_v, rw_w_o, rw_w0, rw_w1, rw_w2, rw_a0, rw_a1, rw_a2, rw_g1, rw_g2, rw_k_k, rw_k_a, rw_r_k, rw_ln_w, rw_ln_b, ffn_w_gate, ffn_w_up, ffn_w_down):
    H = rel_bias.shape[1]
    p = {"norm_g": norm_g, "heads": H}
    p["attn"] = [_attn_params(at_w_in[i], at_q_norm[i], at_kv_norm[i], at_w_uq[i], at_w_ukv[i],
                              at_w_out[i], H) for i in range(at_w_in.shape[0])]
    p["rwkv"] = [_rwkv_params(rw_mu[i], rw_w_r[i], rw_w_k[i], rw_w_v[i], rw_w_o[i], rw_w0[i],
                              rw_w1[i], rw_w2[i], rw_a0[i], rw_a1[i], rw_a2[i], rw_g1[i],
                              rw_g2[i], rw_k_k[i], rw_k_a[i], rw_r_k[i], rw_ln_w[i], rw_ln_b[i])
                 for i in range(rw_mu.shape[0])]
    p["ffn_wg"] = ffn_w_gate.astype(BF16)
    p["ffn_wu"] = ffn_w_up.astype(BF16)
    p["ffn_wd"] = ffn_w_down.astype(BF16)
    p["bias_t"] = dilated_bias_tiles(rel_bias, MXU_DIM)
    p["cos_t"], p["sin_t"] = {}, {}
    for xs in (x_prompt, x_sample):
        B, S = xs.shape[:2]
        p["cos_t"][(B, S)], p["sin_t"][(B, S)] = _rope_tables(B, S)
    return trunk(x_prompt, p), trunk(x_sample, p)
```

```python
import functools
import math

import jax
import jax.numpy as jnp
import numpy as np
from jax import lax
from jax.experimental import pallas as pl
from jax.experimental.pallas import tpu as pltpu

F32 = jnp.float32
BF16 = jnp.bfloat16

MLA_NOPE_DIM = 128
MLA_ROPE_DIM = 64
MLA_V_DIM = 128
ROPE_BASE = 10000.0
DIL_HEAD_DIM = 128
DIL_PATTERNS = ((128, 1), (512, 4), (2048, 16))
N_BUCKETS = 32
T5_MAX_DISTANCE = 1024
RWKV_HEAD = 64
RWKV_LN_EPS = 64e-5
NORM_EPS = 1e-6
NEG_BIG = -1e30
LOG2E = 1.4426950408889634

LANES = 128
SUBLANES = 8
MXU_DIM = 256
VMEM_LIMIT = 56 * 1024 * 1024

SCAN_CHUNK = 64
SCAN_LANES = 2048
SCAN_GROUP = 128


def _cparams(sem, vmem=VMEM_LIMIT):
    return pltpu.CompilerParams(dimension_semantics=sem, vmem_limit_bytes=vmem)


def _rms(x, g):
    return x * lax.rsqrt(jnp.mean(x * x, axis=-1, keepdims=True) + NORM_EPS) * g


def _dot(a, b):
    return jnp.dot(a, b, preferred_element_type=F32)


def _dot_nt(a, b):
    return lax.dot_general(a, b, (((1,), (1,)), ((), ())), preferred_element_type=F32)


def _dot_tn(a, b):
    return lax.dot_general(a, b, (((0,), (0,)), ((), ())), preferred_element_type=F32)


def _sigmoid(x):
    return 1.0 / (1.0 + jnp.exp(-x))


def _fused_mm_kernel(*refs, n_rows, n_vecs, n_extra, n_mix, tiles_per_mix, tiles_per_seq,
                     prologue, epilogue, tail_mixes, tail_epilogue):
    rows = refs[:n_rows]
    vecs = refs[n_rows:n_rows + n_vecs]
    w_ref = refs[n_rows + n_vecs]
    base = n_rows + n_vecs + 1
    extras = refs[base:base + n_extra]
    o_ref = refs[base + n_extra]
    h_ref = refs[base + n_extra + 1]
    j = pl.program_id(1)
    flags = ()
    if tiles_per_seq:
        it = pl.program_id(0) % tiles_per_seq
        flags = (jnp.where(it == 0, 0.0, 1.0), jnp.where(it == tiles_per_seq - 1, 0.0, 1.0))

    @pl.when(j == 0)
    def _():
        def emit(m, value):
            h_ref[m] = value.astype(BF16)

        prologue(emit, *[r[...] for r in rows], *flags, *[v[...] for v in vecs])

    if n_mix == 1:
        epilogue(j, _dot(h_ref[0], w_ref[...]), o_ref, *extras)
    elif not tail_mixes:
        epilogue(j, _dot(h_ref[j // tiles_per_mix], w_ref[...]), o_ref, *extras)
    else:
        last = pl.num_programs(1) - 1

        @pl.when(j < last)
        def _():
            epilogue(j, _dot(h_ref[j // tiles_per_mix], w_ref[...]), o_ref, *extras)

        @pl.when(j == last)
        def _():
            sub = w_ref.shape[1] // len(tail_mixes)
            tail_epilogue([_dot(h_ref[m], w_ref[:, s * sub:(s + 1) * sub])
                           for s, m in enumerate(tail_mixes)], o_ref)


def fused_mm(rows, vecs, w, prologue, epilogue, *, out_dtype, tm, tn, row_cols=None,
             row_width=None, n_mix=1, tiles_per_mix=1, extras=(), seq_len=None, tail_mixes=(),
             tail_epilogue=None, name):
    T = rows[0].shape[0]
    K, N = w.shape
    row_cols = row_cols or [0] * len(rows)
    row_width = row_width or K
    assert T % tm == 0 and N % tn == 0, (T, tm, N, tn)
    in_specs = [pl.BlockSpec((tm, row_width), functools.partial(lambda i, j, c: (i, c), c=c))
                for c in row_cols]
    if seq_len:
        assert len(rows) == 1 and seq_len % tm == 0
        g8, last8 = tm // SUBLANES, T // SUBLANES - 1
        rows = [rows[0]] * 3
        in_specs += [pl.BlockSpec((SUBLANES, K), lambda i, j: (jnp.maximum(i * g8 - 1, 0), 0)),
                     pl.BlockSpec((SUBLANES, K), lambda i, j: (jnp.minimum((i + 1) * g8, last8), 0))]
    in_specs += [pl.BlockSpec(v.shape, lambda i, j: (0, 0)) for v in vecs]
    in_specs += [pl.BlockSpec((K, tn), lambda i, j: (0, j))]
    in_specs += [spec for _, spec in extras]
    kern = functools.partial(
        _fused_mm_kernel, n_rows=len(rows), n_vecs=len(vecs), n_extra=len(extras), n_mix=n_mix,
        tiles_per_mix=tiles_per_mix, tiles_per_seq=seq_len // tm if seq_len else 0,
        prologue=prologue, epilogue=epilogue, tail_mixes=tuple(tail_mixes),
        tail_epilogue=tail_epilogue)
    return pl.pallas_call(
        kern,
        grid=(T // tm, N // tn),
        in_specs=in_specs,
        out_specs=pl.BlockSpec((tm, tn), lambda i, j: (i, j)),
        out_shape=jax.ShapeDtypeStruct((T, N), out_dtype),
        scratch_shapes=[pltpu.VMEM((n_mix, tm, K), BF16)],
        compiler_params=_cparams(("parallel", "arbitrary")),
        name=name,
    )(*rows, *vecs, w, *[a for a, _ in extras])


def _norm_prologue(emit, x, g):
    emit(0, _rms(x, g))


def _store_epilogue(j, acc, o_ref):
    o_ref[...] = acc.astype(o_ref.dtype)


def _scale_epilogue(j, acc, o_ref, s_ref):
    o_ref[...] = (acc * s_ref[...]).astype(o_ref.dtype)


def _mm_post_kernel(a_ref, b_ref, w_ref, g_ref, x_ref, o_ref):
    lhs = jnp.concatenate([a_ref[...], b_ref[...]], axis=-1)
    y = _dot(lhs, w_ref[...])
    o_ref[...] = x_ref[...] + _rms(y, g_ref[...])


def mm_post(a, b, w, g, x, *, tm, name):
    T, Ka = a.shape
    Kb = b.shape[1]
    D = w.shape[1]
    return pl.pallas_call(
        _mm_post_kernel,
        grid=(T // tm,),
        in_specs=[pl.BlockSpec((tm, Ka), lambda i: (i, 0)),
                  pl.BlockSpec((tm, Kb), lambda i: (i, 0)),
                  pl.BlockSpec((Ka + Kb, D), lambda i: (0, 0), pipeline_mode=pl.Buffered(1)),
                  pl.BlockSpec((1, D), lambda i: (0, 0)),
                  pl.BlockSpec((tm, D), lambda i: (i, 0))],
        out_specs=pl.BlockSpec((tm, D), lambda i: (i, 0)),
        out_shape=jax.ShapeDtypeStruct((T, D), F32),
        compiler_params=_cparams(("parallel",)),
        name=name,
    )(a, b, w, g, x)


def _ffn_kernel(x_ref, gpre_ref, gpost_ref, wg_ref, wu_ref, wd_ref, o_ref, h_ref, acc_ref):
    f = pl.program_id(1)

    @pl.when(f == 0)
    def _():
        h_ref[...] = _rms(x_ref[...], gpre_ref[...]).astype(BF16)
        acc_ref[...] = jnp.zeros_like(acc_ref)

    h = h_ref[...]
    gate = _dot(h, wg_ref[...])
    up = _dot(h, wu_ref[...])
    act = (gate * _sigmoid(gate) * up).astype(BF16)
    acc_ref[...] += _dot(act, wd_ref[...])

    @pl.when(f == pl.num_programs(1) - 1)
    def _():
        o_ref[...] = x_ref[...] + _rms(acc_ref[...], gpost_ref[...])


def ffn(x, g_pre, g_post, wg, wu, wd, *, tm, tf):
    T, D = x.shape
    Fh = wg.shape[1]
    assert T % tm == 0 and Fh % tf == 0
    return pl.pallas_call(
        _ffn_kernel,
        grid=(T // tm, Fh // tf),
        in_specs=[pl.BlockSpec((tm, D), lambda i, f: (i, 0)),
                  pl.BlockSpec((1, D), lambda i, f: (0, 0)),
                  pl.BlockSpec((1, D), lambda i, f: (0, 0)),
                  pl.BlockSpec((D, tf), lambda i, f: (0, f)),
                  pl.BlockSpec((D, tf), lambda i, f: (0, f)),
                  pl.BlockSpec((tf, D), lambda i, f: (f, 0))],
        out_specs=pl.BlockSpec((tm, D), lambda i, f: (i, 0)),
        out_shape=jax.ShapeDtypeStruct((T, D), F32),
        scratch_shapes=[pltpu.VMEM((tm, D), BF16), pltpu.VMEM((tm, D), F32)],
        compiler_params=_cparams(("parallel", "arbitrary")),
        name="ffn",
    )(x, g_pre, g_post, wg, wu, wd)


def _rope_groups(x, cos_t, sin_t):
    half = MLA_ROPE_DIM // 2
    n = x.shape[-1] // LANES
    lane = lax.broadcasted_iota(jnp.int32, x.shape, 1) % LANES
    partner = jnp.where(lane < half, pltpu.roll(x, x.shape[-1] - half, 1), pltpu.roll(x, half, 1))
    if n > 1:
        cos_t = jnp.concatenate([cos_t] * n, axis=-1)
        sin_t = jnp.concatenate([sin_t] * n, axis=-1)
    return x * cos_t + partner * sin_t


def _q_epilogue(j, acc, o_ref, cos_ref, sin_ref, *, scale):
    @pl.when(j == 0)
    def _():
        o_ref[...] = (acc * scale).astype(o_ref.dtype)

    @pl.when(j == 1)
    def _():
        o_ref[...] = (_rope_groups(acc, cos_ref[...], sin_ref[...]) * scale).astype(o_ref.dtype)


def _rope_k_kernel(x_ref, cos_ref, sin_ref, o_ref):
    o_ref[...] = _rope_groups(x_ref[...], cos_ref[...], sin_ref[...]).astype(o_ref.dtype)


def rope_k(lat, col_block, cos_t, sin_t, *, tm):
    T = lat.shape[0]
    return pl.pallas_call(
        _rope_k_kernel,
        grid=(T // tm,),
        in_specs=[pl.BlockSpec((tm, LANES), lambda i: (i, col_block)),
                  pl.BlockSpec((tm, LANES), lambda i: (i, 0)),
                  pl.BlockSpec((tm, LANES), lambda i: (i, 0))],
        out_specs=pl.BlockSpec((tm, LANES), lambda i: (i, 0)),
        out_shape=jax.ShapeDtypeStruct((T, LANES), BF16),
        compiler_params=_cparams(("parallel",)),
        name="rope_k",
    )(lat, cos_t, sin_t)


def _softmax_weights(ss, m_prev):
    m_new, ps = [], []
    for h, tiles in enumerate(ss):
        m = m_prev[h]
        for s in tiles:
            m = jnp.maximum(m, jnp.max(s, axis=0, keepdims=True))
        m_new.append(m)
    for h, tiles in enumerate(ss):
        ps.append([jnp.exp2(s - m_new[h]) for s in tiles])
    return m_new, ps


def _weighted_values(ps, vs, l, acc):
    for p, v in zip(ps, vs):
        l = l + jnp.sum(p, axis=0, keepdims=True)
        acc = acc + _dot_tn(v, p.astype(BF16))
    return l, acc


def _mla_kernel(qn_ref, qr_ref, kn_ref, kr_ref, v_ref, o_ref, m_ref, l_ref, acc_ref, sa_ref,
                sb_ref, *, tk):
    nkv = kn_ref.shape[0] // tk
    assert nkv % 2 == 0
    n, _, tq = acc_ref.shape
    q = jnp.concatenate([qn_ref[...], qr_ref[...]], axis=-1)
    qs = [q[h * tq:(h + 1) * tq] for h in range(n)]
    m_ref[...] = jnp.full_like(m_ref, NEG_BIG)
    l_ref[...] = jnp.zeros_like(l_ref)
    acc_ref[...] = jnp.zeros_like(acc_ref)

    def chunk(c):
        return pl.ds(c * tk if isinstance(c, int) else pl.multiple_of(c * tk, tk), tk)

    def put_scores(c, s_ref):
        k = jnp.concatenate([kn_ref[chunk(c), :], kr_ref[chunk(c), :]], axis=-1)
        for h in range(n):
            s_ref[h] = _dot_nt(k, qs[h])

    def consume(c, s_ref):
        v = v_ref[chunk(c), :]
        m_prev = [m_ref[h] for h in range(n)]
        m_new, ps = _softmax_weights([[s_ref[h]] for h in range(n)], m_prev)
        for h in range(n):
            alpha = jnp.exp2(m_prev[h] - m_new[h])
            l, acc = _weighted_values(ps[h], [v], alpha * l_ref[h], alpha * acc_ref[h])
            l_ref[h] = l
            acc_ref[h] = acc
            m_ref[h] = m_new[h]

    put_scores(0, sa_ref)

    def body(i, carry):
        c = 2 * i
        put_scores(c + 1, sb_ref)
        consume(c, sa_ref)
        put_scores(c + 2, sa_ref)
        consume(c + 1, sb_ref)
        return carry

    lax.fori_loop(0, nkv // 2 - 1, body, 0)
    put_scores(nkv - 1, sb_ref)
    consume(nkv - 2, sa_ref)
    consume(nkv - 1, sb_ref)
    for h in range(n):
        o_ref[h * tq:(h + 1) * tq, :] = (acc_ref[h] / l_ref[h]).T.astype(o_ref.dtype)


def mla_attention(q_all, kv, krope, B, S, H, *, tq, n_chain, tk):
    T = B * S
    tb = tq * n_chain
    nq = S // tb
    dv = MLA_V_DIM
    return pl.pallas_call(
        functools.partial(_mla_kernel, tk=tk),
        grid=(B, H, nq),
        in_specs=[pl.BlockSpec((tb, LANES), lambda b, h, i: (b * nq + i, h)),
                  pl.BlockSpec((tb, LANES), lambda b, h, i: (b * nq + i, H + h)),
                  pl.BlockSpec((S, LANES), lambda b, h, i: (b, h)),
                  pl.BlockSpec((S, LANES), lambda b, h, i: (b, 0)),
                  pl.BlockSpec((S, LANES), lambda b, h, i: (b, H + h))],
        out_specs=pl.BlockSpec((tb, dv), lambda b, h, i: (b * nq + i, h)),
        out_shape=jax.ShapeDtypeStruct((T, H * dv), BF16),
        scratch_shapes=[pltpu.VMEM((n_chain, 1, tq), F32), pltpu.VMEM((n_chain, 1, tq), F32),
                        pltpu.VMEM((n_chain, dv, tq), F32), pltpu.VMEM((n_chain, tk, tq), F32),
                        pltpu.VMEM((n_chain, tk, tq), F32)],
        compiler_params=_cparams(("parallel", "parallel", "arbitrary")),
        name="mla_attention",
    )(q_all, q_all, kv, krope, kv)


def _dil_kernel(q_ref, k_ref, v_ref, bias_ref, o_ref, *, t, nside, n_chain):
    nk = k_ref.shape[0] // t
    i = pl.program_id(2)
    ss, vs = [], []
    for h in range(n_chain):
        q = q_ref[h * t:(h + 1) * t, :]
        tiles, vals = [], []
        for d in range(2 * nside + 1):
            c = i * n_chain + h + d - nside
            inside = jnp.logical_and(c >= 0, c < nk)
            rows = pl.ds(pl.multiple_of(jnp.clip(c, 0, nk - 1) * t, t), t)
            tiles.append(_dot_nt(k_ref[rows, :], q)
                         + (bias_ref[d] + jnp.where(inside, 0.0, NEG_BIG)))
            vals.append(v_ref[rows, :])
        ss.append(tiles)
        vs.append(vals)
    m_new, ps = _softmax_weights(ss, [jnp.full((1, t), NEG_BIG, F32)] * n_chain)
    for h in range(n_chain):
        l, acc = _weighted_values(ps[h], vs[h], jnp.zeros((1, t), F32),
                                  jnp.zeros((DIL_HEAD_DIM, t), F32))
        o_ref[h * t:(h + 1) * t, :] = (acc / l).T.astype(o_ref.dtype)


def dilated_attention(qkv, bias_t, B, S, H, *, t, n_chain):
    T = B * S
    tb = t * n_chain
    nq = S // tb
    nd = bias_t.shape[0]
    dh = DIL_HEAD_DIM
    return pl.pallas_call(
        functools.partial(_dil_kernel, t=t, nside=(nd - 1) // 2, n_chain=n_chain),
        grid=(B, H, nq),
        in_specs=[pl.BlockSpec((tb, dh), lambda b, h, i: (b * nq + i, h)),
                  pl.BlockSpec((S, dh), lambda b, h, i: (b, H + h)),
                  pl.BlockSpec((S, dh), lambda b, h, i: (b, 2 * H + h)),
                  pl.BlockSpec((nd, None, t, t), lambda b, h, i: (0, h, 0, 0))],
        out_specs=pl.BlockSpec((tb, dh), lambda b, h, i: (b * nq + i, h)),
        out_shape=jax.ShapeDtypeStruct((T, H * dh), BF16),
        compiler_params=_cparams(("parallel", "parallel", "arbitrary")),
        name="dilated_attention",
    )(qkv, qkv, qkv, bias_t)


def _t5_bucket_np(rel):
    half = N_BUCKETS // 2
    max_exact = half // 2
    bucket = np.where(rel > 0, half, 0)
    n = np.abs(rel)
    nf = np.maximum(n, 1).astype(np.float64)
    large = max_exact + (np.log(nf / max_exact) / math.log(T5_MAX_DISTANCE / max_exact)
                         * (half - max_exact)).astype(np.int64)
    large = np.minimum(large, half - 1)
    return bucket + np.where(n < max_exact, n, large)


def dilated_bias_tiles(rel_bias, t):
    reach = max(w // 2 for w, _ in DIL_PATTERNS)
    nside = -(-reach // t)
    nd = 2 * nside + 1
    span = (nside + 1) * t
    delta = np.arange(-span + 1, span)
    mult = np.zeros(delta.shape, np.int64)
    for window, dil in DIL_PATTERNS:
        mult += ((delta % dil == 0) & (np.abs(delta) <= window // 2)).astype(np.int64)
    logm = np.log(np.maximum(mult, 1)).astype(np.float32)
    f = (rel_bias.astype(F32)[_t5_bucket_np(delta)] + logm[:, None]) * LOG2E
    f = jnp.where((mult > 0)[:, None], f, NEG_BIG)
    H = f.shape[1]
    frev = jnp.pad(f[::-1].T, ((0, 0), (1, 0)))
    wins = []
    for d in range(nd):
        s_d = span - 1 - (d - nside) * t + 1
        wins.append(jnp.concatenate([frev[:, s_d:s_d + t], frev[:, s_d - t:s_d]], axis=1))
    g = jnp.stack(wins, axis=0)
    flat = jnp.tile(g, (1, 1, t))[:, :, :t * (2 * t - 1)]
    return flat.reshape(nd, H, t, 2 * t - 1)[:, :, :, :t]


def _shift_prologue(emit, x, x_before, x_after, keep_before, keep_after, g, mu):
    tm = x.shape[0]
    h = _rms(x, g)
    edge_prev = _rms(x_before[SUBLANES - 1:SUBLANES, :], g) * keep_before
    edge_next = _rms(x_after[0:1, :], g) * keep_after
    row = lax.broadcasted_iota(jnp.int32, h.shape, 0)
    h_prev = jnp.where(row == 0, edge_prev, pltpu.roll(h, 1, 0))
    h_next = jnp.where(row == tm - 1, edge_next, pltpu.roll(h, tm - 1, 0))
    xx = 0.5 * (h_prev + h_next) - h
    for m in range(mu.shape[0]):
        emit(m, h + xx * mu[m:m + 1, :])


def _lora1_tail(accs, o_ref):
    sub = accs[0].shape[1]
    acts = (_sigmoid(accs[0]), jnp.tanh(accs[1]), accs[2], jnp.zeros_like(accs[3]))
    for s, a in enumerate(acts):
        o_ref[:, s * sub:(s + 1) * sub] = a.astype(o_ref.dtype)


def _lora2_kernel(a_ref, w_ref, b_ref, o_ref, *, tiles_per_d):
    j = pl.program_id(1)
    z = _dot(a_ref[...].astype(BF16), w_ref[...]) + b_ref[...]

    @pl.when(j < tiles_per_d)
    def _():
        o_ref[...] = z

    @pl.when(jnp.logical_and(j >= tiles_per_d, j < 3 * tiles_per_d))
    def _():
        w_log = -(jnp.maximum(-z, 0.0) + jnp.log(1.0 + jnp.exp(-jnp.abs(z)))) - 0.5
        o_ref[...] = -jnp.exp(w_log)

    @pl.when(j >= 3 * tiles_per_d)
    def _():
        o_ref[...] = _sigmoid(z)


def lora2(l1, col0, w, bias, *, tm, tn, kb):
    T = l1.shape[0]
    N = w.shape[1]
    D = N // 5
    assert D % tn == 0 and col0 % kb == 0

    def a_map(i, j):
        return (i, col0 // kb + (j * tn + D) // (2 * D))

    return pl.pallas_call(
        functools.partial(_lora2_kernel, tiles_per_d=D // tn),
        grid=(T // tm, N // tn),
        in_specs=[pl.BlockSpec((tm, kb), a_map), pl.BlockSpec((kb, tn), lambda i, j: (0, j)),
                  pl.BlockSpec((1, tn), lambda i, j: (0, j))],
        out_specs=pl.BlockSpec((tm, tn), lambda i, j: (i, j)),
        out_shape=jax.ShapeDtypeStruct((T, N), F32),
        compiler_params=_cparams(("parallel", "parallel")),
        name="rwkv_lora2",
    )(l1, w, bias)


def _head_block_mask(n):
    r = lax.broadcasted_iota(jnp.int32, (n, n), 0) // RWKV_HEAD
    c = lax.broadcasted_iota(jnp.int32, (n, n), 1) // RWKV_HEAD
    return r == c


def _head_sums(x):
    ones = jnp.where(_head_block_mask(MXU_DIM), 1.0, 0.0).astype(BF16)
    parts = [_dot(x[:, g:g + MXU_DIM].astype(BF16), ones) for g in range(0, x.shape[-1], MXU_DIM)]
    return parts[0] if len(parts) == 1 else jnp.concatenate(parts, axis=-1)


def _split3(x):
    hi = x.astype(BF16)
    r1 = x - hi.astype(F32)
    mid = r1.astype(BF16)
    lo = (r1 - mid.astype(F32)).astype(BF16)
    return hi, mid, lo


SCAN_OPERANDS = ("at", "rt", "bt", "kt", "at_s", "rt_s", "bh", "kh", "v")


def _scan_prepare(r_ref, k_ref, v_ref, lw_ref, a_ref, kk_ref, ka_ref, sign):
    C = SCAN_CHUNK
    r = r_ref[...]
    k = k_ref[...]
    kk = k * kk_ref[...]
    kk = kk * lax.rsqrt(_head_sums(kk * kk) + 1e-12)
    lw = lw_ref[...]
    a = a_ref[...]
    kd = k * (1.0 + (a - 1.0) * ka_ref[...])
    aa = -kk
    bb = kk * a

    trow = lax.broadcasted_iota(jnp.int32, (C, C), 0)
    tcol = lax.broadcasted_iota(jnp.int32, (C, C), 1)
    cum = jnp.where((trow - tcol) * sign >= 0, 1.0, 0.0).astype(BF16)
    hi, mid, lo = _split3(lw)
    l_inc = _dot(cum, hi) + _dot(cum, mid) + _dot(cum, lo)
    l_exc = l_inc - lw
    l_ref = l_inc[C // 2:C // 2 + 1, :]
    l_end = l_inc[C - 1:C, :] if sign > 0 else l_inc[0:1, :]
    e_in = jnp.exp(l_inc - l_ref)
    e_ex = jnp.exp(l_exc - l_ref)
    e_ng = jnp.exp(l_ref - l_inc)
    rho = jnp.exp(l_ref)
    end_over_ref = jnp.exp(l_end - l_ref)
    at = aa * e_ex
    rt = r * e_in
    bt = bb * e_ng
    kt = kd * e_ng
    vals = dict(at=at, rt=rt, bt=bt, kt=kt, at_s=at * rho, rt_s=rt * rho, bh=bt * end_over_ref,
                kh=kt * end_over_ref, v=v_ref[...])
    return {name: vals[name].astype(BF16) for name in SCAN_OPERANDS}, jnp.exp(l_end)


def _scan_advance(chains):
    C = SCAN_CHUNK
    HN = RWKV_HEAD
    GW = chains[0]["at"].shape[-1]
    cat = jnp.concatenate
    lane_head = lax.broadcasted_iota(jnp.int32, (1, GW), 1) // HN

    def mstack(x):
        zero = jnp.zeros_like(x)
        return cat([jnp.where(lane_head == j, x, zero) for j in range(GW // HN)], axis=0)

    trow = lax.broadcasted_iota(jnp.int32, (C, GW), 0)
    scol = lax.broadcasted_iota(jnp.int32, (C, GW), 1) % HN
    eye = jnp.where(scol == trow, 1.0, 0.0)
    bdiag = _head_block_mask(GW)
    masks = {}
    for sign in sorted({ch["sign"] for ch in chains}):
        couples = [jnp.logical_and(((trow >> lvl) & 1) - ((scol >> lvl) & 1) == sign,
                                   (trow >> (lvl + 1)) == (scol >> (lvl + 1)))
                   for lvl in range(int(math.log2(C)))]
        masks[sign] = dict(strict=(trow - scol) * sign > 0, incl=(trow - scol) * sign >= 0,
                           couples=couples)
    strict = [masks[ch["sign"]]["strict"] for ch in chains]
    incl = [masks[ch["sign"]]["incl"] for ch in chains]

    sc = [_dot_nt(cat([ch["at"], ch["rt"]], axis=0),
                  cat([mstack(ch["bt"]), mstack(ch["kt"])], axis=0)) for ch in chains]
    a_ab = [jnp.where(m, s[:C, :GW], 0.0) for m, s in zip(strict, sc)]
    a_ak = [jnp.where(m, s[:C, GW:], 0.0) for m, s in zip(strict, sc)]
    a_r = [cat([jnp.where(m, s[C:, :GW], 0.0), jnp.where(m, s[C:, GW:], 0.0)], axis=1).astype(BF16)
           for m, s in zip(incl, sc)]

    av = [_dot(x.astype(BF16), mstack(ch["v"])) for x, ch in zip(a_ak, chains)]
    tinv = None
    for lvl in range(int(math.log2(C))):
        low = [jnp.where(masks[ch["sign"]]["couples"][lvl], x, 0.0) for ch, x in zip(chains, a_ab)]
        if tinv is None:
            tinv = [eye + x for x in low]
            continue
        ld = [_dot(x.astype(BF16), mstack(t.astype(BF16))) for x, t in zip(low, tinv)]
        tinv = [t + _dot(t.astype(BF16), mstack(x.astype(BF16))) for t, x in zip(tinv, ld)]

    uw = [_dot(t.astype(BF16), cat([mstack(x.astype(BF16)), mstack(ch["at_s"])], axis=1))
          for t, x, ch in zip(tinv, av, chains)]

    s_old = [ch["s_ref"][ch["slot"]] for ch in chains]
    us = [_dot_nt(cat([w[:, GW:].astype(BF16), ch["rt_s"]], axis=0), s.astype(BF16))
          for w, ch, s in zip(uw, chains, s_old)]
    u = [x[:C] + w[:, :GW] for x, w in zip(us, uw)]
    y = [x[C:] + _dot(ar, cat([mstack(uu.astype(BF16)), mstack(ch["v"])], axis=0))
         for x, ar, uu, ch in zip(us, a_r, u, chains)]
    for ch, yy in zip(chains, y):
        ch["y_ref"][:, ch["lanes"]] = yy

    s_new = [_dot(cat([uu, ch["v"].astype(F32)], axis=0).T.astype(BF16),
                  cat([ch["bh"], ch["kh"]], axis=0)) for uu, ch in zip(u, chains)]
    for ch, sn, so in zip(chains, s_new, s_old):
        ch["s_ref"][ch["slot"]] = jnp.where(bdiag, sn, 0.0) + so * ch["p_end"]


def _scan_kernel(*refs):
    fwd_in, bwd_in = refs[0:5], refs[5:10]
    kk_ref, ka_ref, yf_ref, yb_ref, s_ref = refs[10:15]

    @pl.when(pl.program_id(2) == 0)
    def _():
        s_ref[...] = jnp.zeros_like(s_ref)

    L = kk_ref.shape[-1]
    gw = min(SCAN_GROUP, L)
    chains = []
    for di, (ins, y_ref, sign) in enumerate(((fwd_in, yf_ref, 1), (bwd_in, yb_ref, -1))):
        ops, p_end = _scan_prepare(*ins, kk_ref, ka_ref, sign)
        for p in range(L // gw):
            lanes = slice(p * gw, (p + 1) * gw)
            ch = {name: ops[name][:, lanes] for name in SCAN_OPERANDS}
            ch.update(sign=sign, p_end=p_end[:, lanes], lanes=lanes, y_ref=y_ref, s_ref=s_ref,
                      slot=di * (L // gw) + p)
            chains.append(ch)
    _scan_advance(chains)


def rwkv_scan(rkv, l2, k_k, k_a, B, S, D):
    T = B * S
    C = SCAN_CHUNK
    L = min(SCAN_LANES, D)
    gw = min(SCAN_GROUP, L)
    nc = S // C
    ng = D // L

    def row(b, c, d):
        return b * nc + (nc - 1 - c if d else c)

    def spec(col, d):
        return pl.BlockSpec((C, L), lambda b, g, c: (row(b, c, d), col * ng + g))

    def dir_specs(d):
        return [spec(0, d), spec(1, d), spec(2, d), spec(1 + d, d), spec(3 + d, d)]

    vec = pl.BlockSpec((1, L), lambda b, g, c: (0, g))
    out = jax.ShapeDtypeStruct((T, D), F32)
    return pl.pallas_call(
        _scan_kernel,
        grid=(B, ng, nc),
        in_specs=dir_specs(0) + dir_specs(1) + [vec, vec],
        out_specs=[pl.BlockSpec((C, L), lambda b, g, c: (row(b, c, 0), g)),
                   pl.BlockSpec((C, L), lambda b, g, c: (row(b, c, 1), g))],
        out_shape=[out, out],
        scratch_shapes=[pltpu.VMEM((2 * L // gw, gw, gw), F32)],
        compiler_params=_cparams(("parallel", "parallel", "arbitrary")),
        name="rwkv_scan",
    )(rkv, rkv, rkv, l2, l2, rkv, rkv, rkv, l2, l2, k_k, k_a)


def _rwkv_out_kernel(yf_ref, yb_ref, r_ref, k_ref, v_ref, g_ref, rk_ref, lnw_ref, lnb_ref, wo_ref,
                     gpost_ref, x_ref, o_ref):
    inv_n = 1.0 / RWKV_HEAD
    y = yf_ref[...] + yb_ref[...]
    mean = _head_sums(y) * inv_n
    yc = y - mean
    var = _head_sums(yc * yc) * inv_n
    yn = yc * lax.rsqrt(var + RWKV_LN_EPS) * lnw_ref[...] + lnb_ref[...]
    bonus = _head_sums(r_ref[...] * k_ref[...] * rk_ref[...]) * v_ref[...]
    mix = ((yn + bonus) * g_ref[...]).astype(BF16)
    out = _dot(mix, wo_ref[...])
    o_ref[...] = x_ref[...] + _rms(out, gpost_ref[...])


def rwkv_out(y_fwd, y_bwd, rkv, l2, r_k, ln_w, ln_b, w_o, g_post, x, *, tm):
    T, D = x.shape
    vec = pl.BlockSpec((1, D), lambda i: (0, 0))
    return pl.pallas_call(
        _rwkv_out_kernel,
        grid=(T // tm,),
        in_specs=[pl.BlockSpec((tm, D), lambda i: (i, 0)),
                  pl.BlockSpec((tm, D), lambda i: (i, 0)),
                  pl.BlockSpec((tm, D), lambda i: (i, 0)),
                  pl.BlockSpec((tm, D), lambda i: (i, 1)),
                  pl.BlockSpec((tm, D), lambda i: (i, 2)),
                  pl.BlockSpec((tm, D), lambda i: (i, 0)),
                  vec, vec, vec,
                  pl.BlockSpec((D, D), lambda i: (0, 0), pipeline_mode=pl.Buffered(1)),
                  vec,
                  pl.BlockSpec((tm, D), lambda i: (i, 0))],
        out_specs=pl.BlockSpec((tm, D), lambda i: (i, 0)),
        out_shape=jax.ShapeDtypeStruct((T, D), F32),
        compiler_params=_cparams(("parallel",)),
        name="rwkv_out",
    )(y_fwd, y_bwd, rkv, rkv, rkv, l2, r_k, ln_w, ln_b, w_o, g_post, x)


def _pad_cols(w, n):
    return jnp.pad(w, ((0, 0), (0, n - w.shape[1])))


def _pad_rows(w, n):
    return jnp.pad(w, ((0, n - w.shape[0]), (0, 0)))


def _attn_params(w_in, q_norm, kv_norm, w_uq, w_ukv, w_out, H):
    qr, kvr = q_norm.shape[0], kv_norm.shape[0]
    rope, nope, vd = MLA_ROPE_DIM, MLA_NOPE_DIM, MLA_V_DIM
    w_lat = jnp.concatenate([w_in[:, :qr + kvr], _pad_cols(w_in[:, qr + kvr:qr + kvr + rope], LANES)],
                            axis=1).astype(BF16)
    w_dil = w_in[:, qr + kvr + rope:].astype(BF16)
    uq = w_uq.reshape(qr, H, nope + rope)
    uq_rope = jnp.pad(uq[:, :, nope:], ((0, 0), (0, 0), (0, LANES - rope)))
    w_q = jnp.concatenate([uq[:, :, :nope].reshape(qr, H * nope), uq_rope.reshape(qr, H * LANES)],
                          axis=1).astype(BF16)
    ukv = w_ukv.reshape(kvr, H, nope + vd)
    w_kv = jnp.concatenate([ukv[:, :, :nope].reshape(kvr, H * nope),
                            ukv[:, :, nope:].reshape(kvr, H * vd)], axis=1).astype(BF16)
    return dict(w_lat=w_lat, w_dil=w_dil, w_q=w_q, w_kv=w_kv, w_out=w_out.astype(BF16),
                q_norm=q_norm[None], kv_norm=kv_norm[None])


def _rope_tables(B, S):
    half = MLA_ROPE_DIM // 2
    inv = ROPE_BASE ** (-jnp.arange(0, MLA_ROPE_DIM, 2, dtype=F32) / MLA_ROPE_DIM)
    ang = jnp.arange(S, dtype=F32)[:, None] * inv[None, :]
    cos, sin = jnp.cos(ang), jnp.sin(ang)
    zeros = jnp.zeros((S, LANES - 2 * half), F32)
    cos_t = jnp.concatenate([cos, cos, zeros], axis=1)
    sin_t = jnp.concatenate([-sin, sin, zeros], axis=1)
    return jnp.tile(cos_t, (B, 1)), jnp.tile(sin_t, (B, 1))


def _rwkv_params(mu, w_r, w_k, w_v, w_o, w0, w1, w2, a0, a1, a2, g1, g2, k_k, k_a, r_k, ln_w,
                 ln_b):
    D = w_r.shape[0]
    kb = MXU_DIM
    w_l1 = jnp.concatenate([_pad_cols(g1, kb), _pad_cols(jnp.concatenate([w1[0], w1[1]], 1), kb),
                            _pad_cols(jnp.concatenate([a1[0], a1[1]], 1), kb),
                            jnp.zeros((D, kb), F32)], axis=1)
    w_proj = jnp.concatenate([w_r, w_k, w_v, w_l1], axis=1).astype(BF16)
    rw = w2.shape[1]
    ra = a2.shape[1]
    z = lambda n: jnp.zeros((n, D), F32)
    w_l2 = jnp.concatenate([
        _pad_rows(g2, kb),
        _pad_rows(w2[0], kb),
        _pad_rows(jnp.concatenate([z(rw), w2[1]], 0), kb),
        _pad_rows(a2[0], kb),
        _pad_rows(jnp.concatenate([z(ra), a2[1]], 0), kb)], axis=1).astype(BF16)
    return dict(w_proj=w_proj, w_l2=w_l2, w_o=w_o.astype(BF16),
                mu=jnp.stack([mu[0], mu[2], mu[3], mu[5], mu[1], mu[4]]),
                b_l2=jnp.concatenate([jnp.zeros((D,), F32), w0[0], w0[1], a0[0], a0[1]])[None],
                k_k=k_k[None], k_a=k_a[None],
                r_k=r_k.reshape(1, D), ln_w=ln_w[None], ln_b=ln_b[None])


def _tile(n, pref):
    t = min(pref, n)
    while n % t:
        t //= 2
    return t


def attention_layer(x, B, S, g_pre, g_post, ap, bias_t, cos_t, sin_t, H):
    T, D = x.shape
    tm = _tile(T, 512)
    qr = ap["q_norm"].shape[1]
    n_lat = ap["w_lat"].shape[1]
    lat = fused_mm([x], [g_pre], ap["w_lat"], _norm_prologue, _store_epilogue, out_dtype=F32,
                   tm=tm, tn=n_lat, name="attn_latent")
    n_dil = ap["w_dil"].shape[1]
    dil_scale = jnp.concatenate([jnp.full((1, n_dil // 3), DIL_HEAD_DIM ** -0.5 * LOG2E, F32),
                                 jnp.ones((1, 2 * n_dil // 3), F32)], axis=1)
    qkv_b = fused_mm([x], [g_pre], ap["w_dil"], _norm_prologue, _scale_epilogue, out_dtype=BF16,
                     tm=tm, tn=n_dil // 3,
                     extras=[(dil_scale, pl.BlockSpec((1, n_dil // 3), lambda i, j: (0, j)))],
                     name="attn_dil_qkv")
    q_scale = (MLA_NOPE_DIM + MLA_ROPE_DIM) ** -0.5 * LOG2E
    nq = ap["w_q"].shape[1]
    tab = pl.BlockSpec((tm, LANES), lambda i, j: (i, 0))
    q_all = fused_mm([lat], [ap["q_norm"]], ap["w_q"], _norm_prologue,
                     functools.partial(_q_epilogue, scale=q_scale), out_dtype=BF16, tm=tm,
                     tn=nq // 2, row_cols=[0], row_width=qr, extras=[(cos_t, tab), (sin_t, tab)],
                     name="mla_q")
    kv = fused_mm([lat], [ap["kv_norm"]], ap["w_kv"], _norm_prologue, _store_epilogue,
                  out_dtype=BF16, tm=tm, tn=ap["w_kv"].shape[1] // 2, row_cols=[1], row_width=qr,
                  name="mla_kv")
    krope = rope_k(lat, (n_lat - LANES) // LANES, cos_t, sin_t, tm=tm)
    a_out = mla_attention(q_all, kv, krope, B, S, H, tq=MXU_DIM, n_chain=4,
                          tk=_tile(S // 2, 1024))
    b_out = dilated_attention(qkv_b, bias_t, B, S, H, t=bias_t.shape[-1], n_chain=2)
    return mm_post(a_out, b_out, ap["w_out"], g_post, x, tm=_tile(T, 256), name="attn_out")


def rwkv_layer(x, B, S, g_pre, g_post, rp):
    T, D = x.shape
    kb = MXU_DIM
    tn = 4 * kb
    assert D % tn == 0
    rkv = fused_mm([x], [g_pre, rp["mu"]], rp["w_proj"], _shift_prologue, _store_epilogue,
                   out_dtype=F32, tm=_tile(S, 512), tn=tn, n_mix=6, tiles_per_mix=D // tn,
                   seq_len=S, tail_mixes=(3, 4, 5, 5), tail_epilogue=_lora1_tail,
                   name="rwkv_proj")
    l2 = lora2(rkv, 3 * D, rp["w_l2"], rp["b_l2"], tm=_tile(T, 512), tn=tn, kb=kb)
    y_fwd, y_bwd = rwkv_scan(rkv, l2, rp["k_k"], rp["k_a"], B, S, D)
    return rwkv_out(y_fwd, y_bwd, rkv, l2, rp["r_k"], rp["ln_w"], rp["ln_b"], rp["w_o"], g_post, x,
                    tm=_tile(T, 256))


def trunk(x3, p):
    B, S, D = x3.shape
    x = x3.reshape(B * S, D)
    depth = p["norm_g"].shape[0]
    for layer in range(depth):
        i = layer // 2
        g = p["norm_g"][layer]
        if layer % 2 == 0:
            x = attention_layer(x, B, S, g[0][None], g[1][None], p["attn"][i], p["bias_t"],
                                p["cos_t"][(B, S)], p["sin_t"][(B, S)], p["heads"])
        else:
            x = rwkv_layer(x, B, S, g[0][None], g[1][None], p["rwkv"][i])
        x = ffn(x, g[2][None], g[3][None], p["ffn_wg"][layer], p["ffn_wu"][layer],
                p["ffn_wd"][layer], tm=_tile(B * S, 512), tf=_tile(p["ffn_wg"][layer].shape[1], 512))
    return x.reshape(B, S, D)


def kernel(x_prompt, x_sample, norm_g, rel_bias, at_w_in, at_q_norm, at_kv_norm, at_w_uq, at_w_ukv, at_w_out, rw_mu, rw_w_r, rw_w_k, rw_w_v, rw_w_o, rw_w0, rw_w1, rw_w2, rw_a0, rw_a1, rw_a2, rw_g1, rw_g2, rw_k_k, rw_k_a, rw_r_k, rw_ln_w, rw_ln_b, ffn_w_gate, ffn_w_up, ffn_w_down):
    H = rel_bias.shape[1]
    p = {"norm_g": norm_g, "heads": H}
    p["attn"] = [_attn_params(at_w_in[i], at_q_norm[i], at_kv_norm[i], at_w_uq[i], at_w_ukv[i],
                              at_w_out[i], H) for i in range(at_w_in.shape[0])]
    p["rwkv"] = [_rwkv_params(rw_mu[i], rw_w_r[i], rw_w_k[i], rw_w_v[i], rw_w_o[i], rw_w0[i],
                              rw_w1[i], rw_w2[i], rw_a0[i], rw_a1[i], rw_a2[i], rw_g1[i],
                              rw_g2[i], rw_k_k[i], rw_k_a[i], rw_r_k[i], rw_ln_w[i], rw_ln_b[i])
                 for i in range(rw_mu.shape[0])]
    p["ffn_wg"] = ffn_w_gate.astype(BF16)
    p["ffn_wu"] = ffn_w_up.astype(BF16)
    p["ffn_wd"] = ffn_w_down.astype(BF16)
    p["bias_t"] = dilated_bias_tiles(rel_bias, MXU_DIM)
    p["cos_t"], p["sin_t"] = {}, {}
    for xs in (x_prompt, x_sample):
        B, S = xs.shape[:2]
        p["cos_t"][(B, S)], p["sin_t"][(B, S)] = _rope_tables(B, S)
    return trunk(x_prompt, p), trunk(x_sample, p)
```

```python
import functools
import math

import jax
import jax.numpy as jnp
import numpy as np
from jax import lax
from jax.experimental import pallas as pl
from jax.experimental.pallas import tpu as pltpu

F32 = jnp.float32
BF16 = jnp.bfloat16

MLA_NOPE_DIM = 128
MLA_ROPE_DIM = 64
MLA_V_DIM = 128
ROPE_BASE = 10000.0
DIL_HEAD_DIM = 128
DIL_PATTERNS = ((128, 1), (512, 4), (2048, 16))
N_BUCKETS = 32
T5_MAX_DISTANCE = 1024
RWKV_HEAD = 64
RWKV_LN_EPS = 64e-5
NORM_EPS = 1e-6
NEG_BIG = -1e30
LOG2E = 1.4426950408889634

LANES = 128
SUBLANES = 8
MXU_DIM = 256
VMEM_LIMIT = 56 * 1024 * 1024

SCAN_CHUNK = 64
SCAN_LANES = 2048
SCAN_GROUP = 128


def _cparams(sem, vmem=VMEM_LIMIT):
    return pltpu.CompilerParams(dimension_semantics=sem, vmem_limit_bytes=vmem)


def _rms(x, g):
    return x * lax.rsqrt(jnp.mean(x * x, axis=-1, keepdims=True) + NORM_EPS) * g


def _dot(a, b):
    return jnp.dot(a, b, preferred_element_type=F32)


def _dot_nt(a, b):
    return lax.dot_general(a, b, (((1,), (1,)), ((), ())), preferred_element_type=F32)


def _dot_tn(a, b):
    return lax.dot_general(a, b, (((0,), (0,)), ((), ())), preferred_element_type=F32)


def _sigmoid(x):
    return 1.0 / (1.0 + jnp.exp(-x))


def _fused_mm_kernel(*refs, n_rows, n_vecs, n_extra, n_mix, tiles_per_mix, tiles_per_seq,
                     prologue, epilogue, tail_mixes, tail_epilogue):
    rows = refs[:n_rows]
    vecs = refs[n_rows:n_rows + n_vecs]
    w_ref = refs[n_rows + n_vecs]
    base = n_rows + n_vecs + 1
    extras = refs[base:base + n_extra]
    o_ref = refs[base + n_extra]
    h_ref = refs[base + n_extra + 1]
    j = pl.program_id(1)
    flags = ()
    if tiles_per_seq:
        it = pl.program_id(0) % tiles_per_seq
        flags = (jnp.where(it == 0, 0.0, 1.0), jnp.where(it == tiles_per_seq - 1, 0.0, 1.0))

    @pl.when(j == 0)
    def _():
        def emit(m, value):
            h_ref[m] = value.astype(BF16)

        prologue(emit, *[r[...] for r in rows], *flags, *[v[...] for v in vecs])

    if n_mix == 1:
        epilogue(j, _dot(h_ref[0], w_ref[...]), o_ref, *extras)
    elif not tail_mixes:
        epilogue(j, _dot(h_ref[j // tiles_per_mix], w_ref[...]), o_ref, *extras)
    else:
        last = pl.num_programs(1) - 1

        @pl.when(j < last)
        def _():
            epilogue(j, _dot(h_ref[j // tiles_per_mix], w_ref[...]), o_ref, *extras)

        @pl.when(j == last)
        def _():
            sub = w_ref.shape[1] // len(tail_mixes)
            tail_epilogue([_dot(h_ref[m], w_ref[:, s * sub:(s + 1) * sub])
                           for s, m in enumerate(tail_mixes)], o_ref)


def fused_mm(rows, vecs, w, prologue, epilogue, *, out_dtype, tm, tn, row_cols=None,
             row_width=None, n_mix=1, tiles_per_mix=1, extras=(), seq_len=None, tail_mixes=(),
             tail_epilogue=None, name):
    T = rows[0].shape[0]
    K, N = w.shape
    row_cols = row_cols or [0] * len(rows)
    row_width = row_width or K
    assert T % tm == 0 and N % tn == 0, (T, tm, N, tn)
    in_specs = [pl.BlockSpec((tm, row_width), functools.partial(lambda i, j, c: (i, c), c=c))
                for c in row_cols]
    if seq_len:
        assert len(rows) == 1 and seq_len % tm == 0
        g8, last8 = tm // SUBLANES, T // SUBLANES - 1
        rows = [rows[0]] * 3
        in_specs += [pl.BlockSpec((SUBLANES, K), lambda i, j: (jnp.maximum(i * g8 - 1, 0), 0)),
                     pl.BlockSpec((SUBLANES, K), lambda i, j: (jnp.minimum((i + 1) * g8, last8), 0))]
    in_specs += [pl.BlockSpec(v.shape, lambda i, j: (0, 0)) for v in vecs]
    in_specs += [pl.BlockSpec((K, tn), lambda i, j: (0, j))]
    in_specs += [spec for _, spec in extras]
    kern = functools.partial(
        _fused_mm_kernel, n_rows=len(rows), n_vecs=len(vecs), n_extra=len(extras), n_mix=n_mix,
        tiles_per_mix=tiles_per_mix, tiles_per_seq=seq_len // tm if seq_len else 0,
        prologue=prologue, epilogue=epilogue, tail_mixes=tuple(tail_mixes),
        tail_epilogue=tail_epilogue)
    return pl.pallas_call(
        kern,
        grid=(T // tm, N // tn),
        in_specs=in_specs,
        out_specs=pl.BlockSpec((tm, tn), lambda i, j: (i, j)),
        out_shape=jax.ShapeDtypeStruct((T, N), out_dtype),
        scratch_shapes=[pltpu.VMEM((n_mix, tm, K), BF16)],
        compiler_params=_cparams(("parallel", "arbitrary")),
        name=name,
    )(*rows, *vecs, w, *[a for a, _ in extras])


def _norm_prologue(emit, x, g):
    emit(0, _rms(x, g))


def _store_epilogue(j, acc, o_ref):
    o_ref[...] = acc.astype(o_ref.dtype)


def _scale_epilogue(j, acc, o_ref, s_ref):
    o_ref[...] = (acc * s_ref[...]).astype(o_ref.dtype)


def _mm_post_kernel(a_ref, b_ref, w_ref, g_ref, x_ref, o_ref):
    lhs = jnp.concatenate([a_ref[...], b_ref[...]], axis=-1)
    y = _dot(lhs, w_ref[...])
    o_ref[...] = x_ref[...] + _rms(y, g_ref[...])


def mm_post(a, b, w, g, x, *, tm, name):
    T, Ka = a.shape
    Kb = b.shape[1]
    D = w.shape[1]
    return pl.pallas_call(
        _mm_post_kernel,
        grid=(T // tm,),
        in_specs=[pl.BlockSpec((tm, Ka), lambda i: (i, 0)),
                  pl.BlockSpec((tm, Kb), lambda i: (i, 0)),
                  pl.BlockSpec((Ka + Kb, D), lambda i: (0, 0), pipeline_mode=pl.Buffered(1)),
                  pl.BlockSpec((1, D), lambda i: (0, 0)),
                  pl.BlockSpec((tm, D), lambda i: (i, 0))],
        out_specs=pl.BlockSpec((tm, D), lambda i: (i, 0)),
        out_shape=jax.ShapeDtypeStruct((T, D), F32),
        compiler_params=_cparams(("parallel",)),
        name=name,
    )(a, b, w, g, x)


def _ffn_kernel(x_ref, gpre_ref, gpost_ref, wg_ref, wu_ref, wd_ref, o_ref, h_ref, acc_ref):
    f = pl.program_id(1)

    @pl.when(f == 0)
    def _():
        h_ref[...] = _rms(x_ref[...], gpre_ref[...]).astype(BF16)
        acc_ref[...] = jnp.zeros_like(acc_ref)

    h = h_ref[...]
    gate = _dot(h, wg_ref[...])
    up = _dot(h, wu_ref[...])
    act = (gate * _sigmoid(gate) * up).astype(BF16)
    acc_ref[...] += _dot(act, wd_ref[...])

    @pl.when(f == pl.num_programs(1) - 1)
    def _():
        o_ref[...] = x_ref[...] + _rms(acc_ref[...], gpost_ref[...])


def ffn(x, g_pre, g_post, wg, wu, wd, *, tm, tf):
    T, D = x.shape
    Fh = wg.shape[1]
    assert T % tm == 0 and Fh % tf == 0
    return pl.pallas_call(
        _ffn_kernel,
        grid=(T // tm, Fh // tf),
        in_specs=[pl.BlockSpec((tm, D), lambda i, f: (i, 0)),
                  pl.BlockSpec((1, D), lambda i, f: (0, 0)),
                  pl.BlockSpec((1, D), lambda i, f: (0, 0)),
                  pl.BlockSpec((D, tf), lambda i, f: (0, f)),
                  pl.BlockSpec((D, tf), lambda i, f: (0, f)),
                  pl.BlockSpec((tf, D), lambda i, f: (f, 0))],
        out_specs=pl.BlockSpec((tm, D), lambda i, f: (i, 0)),
        out_shape=jax.ShapeDtypeStruct((T, D), F32),
        scratch_shapes=[pltpu.VMEM((tm, D), BF16), pltpu.VMEM((tm, D), F32)],
        compiler_params=_cparams(("parallel", "arbitrary")),
        name="ffn",
    )(x, g_pre, g_post, wg, wu, wd)


def _rope_groups(x, cos_t, sin_t):
    half = MLA_ROPE_DIM // 2
    n = x.shape[-1] // LANES
    lane = lax.broadcasted_iota(jnp.int32, x.shape, 1) % LANES
    partner = jnp.where(lane < half, pltpu.roll(x, x.shape[-1] - half, 1), pltpu.roll(x, half, 1))
    if n > 1:
        cos_t = jnp.concatenate([cos_t] * n, axis=-1)
        sin_t = jnp.concatenate([sin_t] * n, axis=-1)
    return x * cos_t + partner * sin_t


def _q_epilogue(j, acc, o_ref, cos_ref, sin_ref, *, scale):
    @pl.when(j == 0)
    def _():
        o_ref[...] = (acc * scale).astype(o_ref.dtype)

    @pl.when(j == 1)
    def _():
        o_ref[...] = (_rope_groups(acc, cos_ref[...], sin_ref[...]) * scale).astype(o_ref.dtype)


def _rope_k_kernel(x_ref, cos_ref, sin_ref, o_ref):
    o_ref[...] = _rope_groups(x_ref[...], cos_ref[...], sin_ref[...]).astype(o_ref.dtype)


def rope_k(lat, col_block, cos_t, sin_t, *, tm):
    T = lat.shape[0]
    return pl.pallas_call(
        _rope_k_kernel,
        grid=(T // tm,),
        in_specs=[pl.BlockSpec((tm, LANES), lambda i: (i, col_block)),
                  pl.BlockSpec((tm, LANES), lambda i: (i, 0)),
                  pl.BlockSpec((tm, LANES), lambda i: (i, 0))],
        out_specs=pl.BlockSpec((tm, LANES), lambda i: (i, 0)),
        out_shape=jax.ShapeDtypeStruct((T, LANES), BF16),
        compiler_params=_cparams(("parallel",)),
        name="rope_k",
    )(lat, cos_t, sin_t)


def _softmax_weights(ss, m_prev):
    m_new, ps = [], []
    for h, tiles in enumerate(ss):
        m = m_prev[h]
        for s in tiles:
            m = jnp.maximum(m, jnp.max(s, axis=0, keepdims=True))
        m_new.append(m)
    for h, tiles in enumerate(ss):
        ps.append([jnp.exp2(s - m_new[h]) for s in tiles])
    return m_new, ps


def _weighted_values(ps, vs, l, acc):
    for p, v in zip(ps, vs):
        l = l + jnp.sum(p, axis=0, keepdims=True)
        acc = acc + _dot_tn(v, p.astype(BF16))
    return l, acc


def _mla_kernel(qn_ref, qr_ref, kn_ref, kr_ref, v_ref, o_ref, m_ref, l_ref, acc_ref, sa_ref,
                sb_ref, *, tk):
    nkv = kn_ref.shape[0] // tk
    assert nkv % 2 == 0
    n, _, tq = acc_ref.shape
    q = jnp.concatenate([qn_ref[...], qr_ref[...]], axis=-1)
    qs = [q[h * tq:(h + 1) * tq] for h in range(n)]
    m_ref[...] = jnp.full_like(m_ref, NEG_BIG)
    l_ref[...] = jnp.zeros_like(l_ref)
    acc_ref[...] = jnp.zeros_like(acc_ref)

    def chunk(c):
        return pl.ds(c * tk if isinstance(c, int) else pl.multiple_of(c * tk, tk), tk)

    def put_scores(c, s_ref):
        k = jnp.concatenate([kn_ref[chunk(c), :], kr_ref[chunk(c), :]], axis=-1)
        for h in range(n):
            s_ref[h] = _dot_nt(k, qs[h])

    def consume(c, s_ref):
        v = v_ref[chunk(c), :]
        m_prev = [m_ref[h] for h in range(n)]
        m_new, ps = _softmax_weights([[s_ref[h]] for h in range(n)], m_prev)
        for h in range(n):
            alpha = jnp.exp2(m_prev[h] - m_new[h])
            l, acc = _weighted_values(ps[h], [v], alpha * l_ref[h], alpha * acc_ref[h])
            l_ref[h] = l
            acc_ref[h] = acc
            m_ref[h] = m_new[h]

    put_scores(0, sa_ref)

    def body(i, carry):
        c = 2 * i
        put_scores(c + 1, sb_ref)
        consume(c, sa_ref)
        put_scores(c + 2, sa_ref)
        consume(c + 1, sb_ref)
        return carry

    lax.fori_loop(0, nkv // 2 - 1, body, 0)
    put_scores(nkv - 1, sb_ref)
    consume(nkv - 2, sa_ref)
    consume(nkv - 1, sb_ref)
    for h in range(n):
        o_ref[h * tq:(h + 1) * tq, :] = (acc_ref[h] / l_ref[h]).T.astype(o_ref.dtype)


def mla_attention(q_all, kv, krope, B, S, H, *, tq, n_chain, tk):
    T = B * S
    tb = tq * n_chain
    nq = S // tb
    dv = MLA_V_DIM
    return pl.pallas_call(
        functools.partial(_mla_kernel, tk=tk),
        grid=(B, H, nq),
        in_specs=[pl.BlockSpec((tb, LANES), lambda b, h, i: (b * nq + i, h)),
                  pl.BlockSpec((tb, LANES), lambda b, h, i: (b * nq + i, H + h)),
                  pl.BlockSpec((S, LANES), lambda b, h, i: (b, h)),
                  pl.BlockSpec((S, LANES), lambda b, h, i: (b, 0)),
                  pl.BlockSpec((S, LANES), lambda b, h, i: (b, H + h))],
        out_specs=pl.BlockSpec((tb, dv), lambda b, h, i: (b * nq + i, h)),
        out_shape=jax.ShapeDtypeStruct((T, H * dv), BF16),
        scratch_shapes=[pltpu.VMEM((n_chain, 1, tq), F32), pltpu.VMEM((n_chain, 1, tq), F32),
                        pltpu.VMEM((n_chain, dv, tq), F32), pltpu.VMEM((n_chain, tk, tq), F32),
                        pltpu.VMEM((n_chain, tk, tq), F32)],
        compiler_params=_cparams(("parallel", "parallel", "arbitrary")),
        name="mla_attention",
    )(q_all, q_all, kv, krope, kv)


def _dil_kernel(q_ref, k_ref, v_ref, bias_ref, o_ref, *, t, nside, n_chain):
    nk = k_ref.shape[0] // t
    i = pl.program_id(2)
    ss, vs = [], []
    for h in range(n_chain):
        q = q_ref[h * t:(h + 1) * t, :]
        tiles, vals = [], []
        for d in range(2 * nside + 1):
            c = i * n_chain + h + d - nside
            inside = jnp.logical_and(c >= 0, c < nk)
            rows = pl.ds(pl.multiple_of(jnp.clip(c, 0, nk - 1) * t, t), t)
            tiles.append(_dot_nt(k_ref[rows, :], q)
                         + (bias_ref[d] + jnp.where(inside, 0.0, NEG_BIG)))
            vals.append(v_ref[rows, :])
        ss.append(tiles)
        vs.append(vals)
    m_new, ps = _softmax_weights(ss, [jnp.full((1, t), NEG_BIG, F32)] * n_chain)
    for h in range(n_chain):
        l, acc = _weighted_values(ps[h], vs[h], jnp.zeros((1, t), F32),
                                  jnp.zeros((DIL_HEAD_DIM, t), F32))
        o_ref[h * t:(h + 1) * t, :] = (acc / l).T.astype(o_ref.dtype)


def dilated_attention(qkv, bias_t, B, S, H, *, t, n_chain):
    T = B * S
    tb = t * n_chain
    nq = S // tb
    nd = bias_t.shape[0]
    dh = DIL_HEAD_DIM
    return pl.pallas_call(
        functools.partial(_dil_kernel, t=t, nside=(nd - 1) // 2, n_chain=n_chain),
        grid=(B, H, nq),
        in_specs=[pl.BlockSpec((tb, dh), lambda b, h, i: (b * nq + i, h)),
                  pl.BlockSpec((S, dh), lambda b, h, i: (b, H + h)),
                  pl.BlockSpec((S, dh), lambda b, h, i: (b, 2 * H + h)),
                  pl.BlockSpec((nd, None, t, t), lambda b, h, i: (0, h, 0, 0))],
        out_specs=pl.BlockSpec((tb, dh), lambda b, h, i: (b * nq + i, h)),
        out_shape=jax.ShapeDtypeStruct((T, H * dh), BF16),
        compiler_params=_cparams(("parallel", "parallel", "arbitrary")),
        name="dilated_attention",
    )(qkv, qkv, qkv, bias_t)


def _t5_bucket_np(rel):
    half = N_BUCKETS // 2
    max_exact = half // 2
    bucket = np.where(rel > 0, half, 0)
    n = np.abs(rel)
    nf = np.maximum(n, 1).astype(np.float64)
    large = max_exact + (np.log(nf / max_exact) / math.log(T5_MAX_DISTANCE / max_exact)
                         * (half - max_exact)).astype(np.int64)
    large = np.minimum(large, half - 1)
    return bucket + np.where(n < max_exact, n, large)


def dilated_bias_tiles(rel_bias, t):
    reach = max(w // 2 for w, _ in DIL_PATTERNS)
    nside = -(-reach // t)
    nd = 2 * nside + 1
    span = (nside + 1) * t
    delta = np.arange(-span + 1, span)
    mult = np.zeros(delta.shape, np.int64)
    for window, dil in DIL_PATTERNS:
        mult += ((delta % dil == 0) & (np.abs(delta) <= window // 2)).astype(np.int64)
    logm = np.log(np.maximum(mult, 1)).astype(np.float32)
    f = (rel_bias.astype(F32)[_t5_bucket_np(delta)] + logm[:, None]) * LOG2E
    f = jnp.where((mult > 0)[:, None], f, NEG_BIG)
    H = f.shape[1]
    frev = jnp.pad(f[::-1].T, ((0, 0), (1, 0)))
    wins = []
    for d in range(nd):
        s_d = span - 1 - (d - nside) * t + 1
        wins.append(jnp.concatenate([frev[:, s_d:s_d + t], frev[:, s_d - t:s_d]], axis=1))
    g = jnp.stack(wins, axis=0)
    flat = jnp.tile(g, (1, 1, t))[:, :, :t * (2 * t - 1)]
    return flat.reshape(nd, H, t, 2 * t - 1)[:, :, :, :t]


def _shift_prologue(emit, x, x_before, x_after, keep_before, keep_after, g, mu):
    tm = x.shape[0]
    h = _rms(x, g)
    edge_prev = _rms(x_before[SUBLANES - 1:SUBLANES, :], g) * keep_before
    edge_next = _rms(x_after[0:1, :], g) * keep_after
    row = lax.broadcasted_iota(jnp.int32, h.shape, 0)
    h_prev = jnp.where(row == 0, edge_prev, pltpu.roll(h, 1, 0))
    h_next = jnp.where(row == tm - 1, edge_next, pltpu.roll(h, tm - 1, 0))
    xx = 0.5 * (h_prev + h_next) - h
    for m in range(mu.shape[0]):
        emit(m, h + xx * mu[m:m + 1, :])


def _lora1_tail(accs, o_ref):
    sub = accs[0].shape[1]
    acts = (_sigmoid(accs[0]), jnp.tanh(accs[1]), accs[2], jnp.zeros_like(accs[3]))
    for s, a in enumerate(acts):
        o_ref[:, s * sub:(s + 1) * sub] = a.astype(o_ref.dtype)


def _lora2_kernel(a_ref, w_ref, b_ref, o_ref, *, tiles_per_d):
    j = pl.program_id(1)
    z = _dot(a_ref[...].astype(BF16), w_ref[...]) + b_ref[...]

    @pl.when(j < tiles_per_d)
    def _():
        o_ref[...] = z

    @pl.when(jnp.logical_and(j >= tiles_per_d, j < 3 * tiles_per_d))
    def _():
        o_ref[...] = -math.exp(-0.5) * _sigmoid(z)

    @pl.when(j >= 3 * tiles_per_d)
    def _():
        o_ref[...] = _sigmoid(z)


def lora2(l1, col0, w, bias, *, tm, tn, kb):
    T = l1.shape[0]
    N = w.shape[1]
    D = N // 5
    assert D % tn == 0 and col0 % kb == 0

    def a_map(i, j):
        return (i, col0 // kb + (j * tn + D) // (2 * D))

    return pl.pallas_call(
        functools.partial(_lora2_kernel, tiles_per_d=D // tn),
        grid=(T // tm, N // tn),
        in_specs=[pl.BlockSpec((tm, kb), a_map), pl.BlockSpec((kb, tn), lambda i, j: (0, j)),
                  pl.BlockSpec((1, tn), lambda i, j: (0, j))],
        out_specs=pl.BlockSpec((tm, tn), lambda i, j: (i, j)),
        out_shape=jax.ShapeDtypeStruct((T, N), F32),
        compiler_params=_cparams(("parallel", "parallel")),
        name="rwkv_lora2",
    )(l1, w, bias)


def _head_block_mask(n):
    r = lax.broadcasted_iota(jnp.int32, (n, n), 0) // RWKV_HEAD
    c = lax.broadcasted_iota(jnp.int32, (n, n), 1) // RWKV_HEAD
    return r == c


def _head_sums(x):
    ones = jnp.where(_head_block_mask(MXU_DIM), 1.0, 0.0).astype(BF16)
    parts = [_dot(x[:, g:g + MXU_DIM].astype(BF16), ones) for g in range(0, x.shape[-1], MXU_DIM)]
    return parts[0] if len(parts) == 1 else jnp.concatenate(parts, axis=-1)


def _split3(x):
    hi = x.astype(BF16)
    r1 = x - hi.astype(F32)
    mid = r1.astype(BF16)
    lo = (r1 - mid.astype(F32)).astype(BF16)
    return hi, mid, lo


SCAN_OPERANDS = ("at", "rt", "bt", "kt", "at_s", "rt_s", "bh", "kh", "v")


def _scan_prepare(r_ref, k_ref, v_ref, lw_ref, a_ref, kk_ref, ka_ref, sign):
    C = SCAN_CHUNK
    r = r_ref[...]
    k = k_ref[...]
    kk = k * kk_ref[...]
    kk = kk * lax.rsqrt(_head_sums(kk * kk) + 1e-12)
    lw = lw_ref[...]
    a = a_ref[...]
    kd = k * (1.0 + (a - 1.0) * ka_ref[...])
    aa = -kk
    bb = kk * a

    trow = lax.broadcasted_iota(jnp.int32, (C, C), 0)
    tcol = lax.broadcasted_iota(jnp.int32, (C, C), 1)
    cum = jnp.where((trow - tcol) * sign >= 0, 1.0, 0.0).astype(BF16)
    hi, mid, lo = _split3(lw)
    l_inc = _dot(cum, hi) + _dot(cum, mid) + _dot(cum, lo)
    l_exc = l_inc - lw
    l_ref = l_inc[C // 2:C // 2 + 1, :]
    l_end = l_inc[C - 1:C, :] if sign > 0 else l_inc[0:1, :]
    e_in = jnp.exp(l_inc - l_ref)
    e_ex = jnp.exp(l_exc - l_ref)
    e_ng = jnp.exp(l_ref - l_inc)
    rho = jnp.exp(l_ref)
    end_over_ref = jnp.exp(l_end - l_ref)
    at = aa * e_ex
    rt = r * e_in
    bt = bb * e_ng
    kt = kd * e_ng
    vals = dict(at=at, rt=rt, bt=bt, kt=kt, at_s=at * rho, rt_s=rt * rho, bh=bt * end_over_ref,
                kh=kt * end_over_ref, v=v_ref[...])
    return {name: vals[name].astype(BF16) for name in SCAN_OPERANDS}, jnp.exp(l_end)


def _scan_advance(chains):
    C = SCAN_CHUNK
    HN = RWKV_HEAD
    GW = chains[0]["at"].shape[-1]
    cat = jnp.concatenate
    lane_head = lax.broadcasted_iota(jnp.int32, (1, GW), 1) // HN

    def mstack(x):
        zero = jnp.zeros_like(x)
        return cat([jnp.where(lane_head == j, x, zero) for j in range(GW // HN)], axis=0)

    trow = lax.broadcasted_iota(jnp.int32, (C, GW), 0)
    scol = lax.broadcasted_iota(jnp.int32, (C, GW), 1) % HN
    eye = jnp.where(scol == trow, 1.0, 0.0)
    bdiag = _head_block_mask(GW)
    masks = {}
    for sign in sorted({ch["sign"] for ch in chains}):
        couples = [jnp.logical_and(((trow >> lvl) & 1) - ((scol >> lvl) & 1) == sign,
                                   (trow >> (lvl + 1)) == (scol >> (lvl + 1)))
                   for lvl in range(int(math.log2(C)))]
        masks[sign] = dict(strict=(trow - scol) * sign > 0, incl=(trow - scol) * sign >= 0,
                           couples=couples)
    strict = [masks[ch["sign"]]["strict"] for ch in chains]
    incl = [masks[ch["sign"]]["incl"] for ch in chains]

    sc = [_dot_nt(cat([ch["at"], ch["rt"]], axis=0),
                  cat([mstack(ch["bt"]), mstack(ch["kt"])], axis=0)) for ch in chains]
    a_ab = [jnp.where(m, s[:C, :GW], 0.0) for m, s in zip(strict, sc)]
    a_ak = [jnp.where(m, s[:C, GW:], 0.0) for m, s in zip(strict, sc)]
    a_r = [cat([jnp.where(m, s[C:, :GW], 0.0), jnp.where(m, s[C:, GW:], 0.0)], axis=1).astype(BF16)
           for m, s in zip(incl, sc)]

    av = [_dot(x.astype(BF16), mstack(ch["v"])) for x, ch in zip(a_ak, chains)]
    tinv = None
    for lvl in range(int(math.log2(C))):
        low = [jnp.where(masks[ch["sign"]]["couples"][lvl], x, 0.0) for ch, x in zip(chains, a_ab)]
        if tinv is None:
            tinv = [eye + x for x in low]
            continue
        ld = [_dot(x.astype(BF16), mstack(t.astype(BF16))) for x, t in zip(low, tinv)]
        tinv = [t + _dot(t.astype(BF16), mstack(x.astype(BF16))) for t, x in zip(tinv, ld)]

    uw = [_dot(t.astype(BF16), cat([mstack(x.astype(BF16)), mstack(ch["at_s"])], axis=1))
          for t, x, ch in zip(tinv, av, chains)]

    s_old = [ch["s_ref"][ch["slot"]] for ch in chains]
    us = [_dot_nt(cat([w[:, GW:].astype(BF16), ch["rt_s"]], axis=0), s.astype(BF16))
          for w, ch, s in zip(uw, chains, s_old)]
    u = [x[:C] + w[:, :GW] for x, w in zip(us, uw)]
    y = [x[C:] + _dot(ar, cat([mstack(uu.astype(BF16)), mstack(ch["v"])], axis=0))
         for x, ar, uu, ch in zip(us, a_r, u, chains)]
    for ch, yy in zip(chains, y):
        ch["y_ref"][:, ch["lanes"]] = yy

    s_new = [_dot(cat([uu, ch["v"].astype(F32)], axis=0).T.astype(BF16),
                  cat([ch["bh"], ch["kh"]], axis=0)) for uu, ch in zip(u, chains)]
    for ch, sn, so in zip(chains, s_new, s_old):
        ch["s_ref"][ch["slot"]] = jnp.where(bdiag, sn, 0.0) + so * ch["p_end"]


def _scan_kernel(*refs):
    fwd_in, bwd_in = refs[0:5], refs[5:10]
    kk_ref, ka_ref, yf_ref, yb_ref, s_ref = refs[10:15]

    @pl.when(pl.program_id(2) == 0)
    def _():
        s_ref[...] = jnp.zeros_like(s_ref)

    L = kk_ref.shape[-1]
    gw = min(SCAN_GROUP, L)
    chains = []
    for di, (ins, y_ref, sign) in enumerate(((fwd_in, yf_ref, 1), (bwd_in, yb_ref, -1))):
        ops, p_end = _scan_prepare(*ins, kk_ref, ka_ref, sign)
        for p in range(L // gw):
            lanes = slice(p * gw, (p + 1) * gw)
            ch = {name: ops[name][:, lanes] for name in SCAN_OPERANDS}
            ch.update(sign=sign, p_end=p_end[:, lanes], lanes=lanes, y_ref=y_ref, s_ref=s_ref,
                      slot=di * (L // gw) + p)
            chains.append(ch)
    _scan_advance(chains)


def rwkv_scan(rkv, l2, k_k, k_a, B, S, D):
    T = B * S
    C = SCAN_CHUNK
    L = min(SCAN_LANES, D)
    gw = min(SCAN_GROUP, L)
    nc = S // C
    ng = D // L

    def row(b, c, d):
        return b * nc + (nc - 1 - c if d else c)

    def spec(col, d):
        return pl.BlockSpec((C, L), lambda b, g, c: (row(b, c, d), col * ng + g))

    def dir_specs(d):
        return [spec(0, d), spec(1, d), spec(2, d), spec(1 + d, d), spec(3 + d, d)]

    vec = pl.BlockSpec((1, L), lambda b, g, c: (0, g))
    out = jax.ShapeDtypeStruct((T, D), F32)
    return pl.pallas_call(
        _scan_kernel,
        grid=(B, ng, nc),
        in_specs=dir_specs(0) + dir_specs(1) + [vec, vec],
        out_specs=[pl.BlockSpec((C, L), lambda b, g, c: (row(b, c, 0), g)),
                   pl.BlockSpec((C, L), lambda b, g, c: (row(b, c, 1), g))],
        out_shape=[out, out],
        scratch_shapes=[pltpu.VMEM((2 * L // gw, gw, gw), F32)],
        compiler_params=_cparams(("parallel", "parallel", "arbitrary")),
        name="rwkv_scan",
    )(rkv, rkv, rkv, l2, l2, rkv, rkv, rkv, l2, l2, k_k, k_a)


def _rwkv_out_kernel(yf_ref, yb_ref, r_ref, k_ref, v_ref, g_ref, rk_ref, lnw_ref, lnb_ref, wo_ref,
                     gpost_ref, x_ref, o_ref):
    inv_n = 1.0 / RWKV_HEAD
    y = yf_ref[...] + yb_ref[...]
    mean = _head_sums(y) * inv_n
    yc = y - mean
    var = _head_sums(yc * yc) * inv_n
    yn = yc * lax.rsqrt(var + RWKV_LN_EPS) * lnw_ref[...] + lnb_ref[...]
    bonus = _head_sums(r_ref[...] * k_ref[...] * rk_ref[...]) * v_ref[...]
    mix = ((yn + bonus) * g_ref[...]).astype(BF16)
    out = _dot(mix, wo_ref[...])
    o_ref[...] = x_ref[...] + _rms(out, gpost_ref[...])


def rwkv_out(y_fwd, y_bwd, rkv, l2, r_k, ln_w, ln_b, w_o, g_post, x, *, tm):
    T, D = x.shape
    vec = pl.BlockSpec((1, D), lambda i: (0, 0))
    return pl.pallas_call(
        _rwkv_out_kernel,
        grid=(T // tm,),
        in_specs=[pl.BlockSpec((tm, D), lambda i: (i, 0)),
                  pl.BlockSpec((tm, D), lambda i: (i, 0)),
                  pl.BlockSpec((tm, D), lambda i: (i, 0)),
                  pl.BlockSpec((tm, D), lambda i: (i, 1)),
                  pl.BlockSpec((tm, D), lambda i: (i, 2)),
                  pl.BlockSpec((tm, D), lambda i: (i, 0)),
                  vec, vec, vec,
                  pl.BlockSpec((D, D), lambda i: (0, 0), pipeline_mode=pl.Buffered(1)),
                  vec,
                  pl.BlockSpec((tm, D), lambda i: (i, 0))],
        out_specs=pl.BlockSpec((tm, D), lambda i: (i, 0)),
        out_shape=jax.ShapeDtypeStruct((T, D), F32),
        compiler_params=_cparams(("parallel",)),
        name="rwkv_out",
    )(y_fwd, y_bwd, rkv, rkv, rkv, l2, r_k, ln_w, ln_b, w_o, g_post, x)


def _pad_cols(w, n):
    return jnp.pad(w, ((0, 0), (0, n - w.shape[1])))


def _pad_rows(w, n):
    return jnp.pad(w, ((0, n - w.shape[0]), (0, 0)))


def _attn_params(w_in, q_norm, kv_norm, w_uq, w_ukv, w_out, H):
    qr, kvr = q_norm.shape[0], kv_norm.shape[0]
    rope, nope, vd = MLA_ROPE_DIM, MLA_NOPE_DIM, MLA_V_DIM
    w_lat = jnp.concatenate([w_in[:, :qr + kvr], _pad_cols(w_in[:, qr + kvr:qr + kvr + rope], LANES)],
                            axis=1).astype(BF16)
    w_dil = w_in[:, qr + kvr + rope:].astype(BF16)
    uq = w_uq.reshape(qr, H, nope + rope)
    uq_rope = jnp.pad(uq[:, :, nope:], ((0, 0), (0, 0), (0, LANES - rope)))
    w_q = jnp.concatenate([uq[:, :, :nope].reshape(qr, H * nope), uq_rope.reshape(qr, H * LANES)],
                          axis=1).astype(BF16)
    ukv = w_ukv.reshape(kvr, H, nope + vd)
    w_kv = jnp.concatenate([ukv[:, :, :nope].reshape(kvr, H * nope),
                            ukv[:, :, nope:].reshape(kvr, H * vd)], axis=1).astype(BF16)
    return dict(w_lat=w_lat, w_dil=w_dil, w_q=w_q, w_kv=w_kv, w_out=w_out.astype(BF16),
                q_norm=q_norm[None], kv_norm=kv_norm[None])


def _rope_tables(B, S):
    half = MLA_ROPE_DIM // 2
    inv = ROPE_BASE ** (-jnp.arange(0, MLA_ROPE_DIM, 2, dtype=F32) / MLA_ROPE_DIM)
    ang = jnp.arange(S, dtype=F32)[:, None] * inv[None, :]
    cos, sin = jnp.cos(ang), jnp.sin(ang)
    zeros = jnp.zeros((S, LANES - 2 * half), F32)
    cos_t = jnp.concatenate([cos, cos, zeros], axis=1)
    sin_t = jnp.concatenate([-sin, sin, zeros], axis=1)
    return jnp.tile(cos_t, (B, 1)), jnp.tile(sin_t, (B, 1))


def _rwkv_params(mu, w_r, w_k, w_v, w_o, w0, w1, w2, a0, a1, a2, g1, g2, k_k, k_a, r_k, ln_w,
                 ln_b):
    D = w_r.shape[0]
    kb = MXU_DIM
    w_l1 = jnp.concatenate([_pad_cols(g1, kb), _pad_cols(jnp.concatenate([w1[0], w1[1]], 1), kb),
                            _pad_cols(jnp.concatenate([a1[0], a1[1]], 1), kb),
                            jnp.zeros((D, kb), F32)], axis=1)
    w_proj = jnp.concatenate([w_r, w_k, w_v, w_l1], axis=1).astype(BF16)
    rw = w2.shape[1]
    ra = a2.shape[1]
    z = lambda n: jnp.zeros((n, D), F32)
    w_l2 = jnp.concatenate([
        _pad_rows(g2, kb),
        _pad_rows(w2[0], kb),
        _pad_rows(jnp.concatenate([z(rw), w2[1]], 0), kb),
        _pad_rows(a2[0], kb),
        _pad_rows(jnp.concatenate([z(ra), a2[1]], 0), kb)], axis=1).astype(BF16)
    return dict(w_proj=w_proj, w_l2=w_l2, w_o=w_o.astype(BF16),
                mu=jnp.stack([mu[0], mu[2], mu[3], mu[5], mu[1], mu[4]]),
                b_l2=jnp.concatenate([jnp.zeros((D,), F32), w0[0], w0[1], a0[0], a0[1]])[None],
                k_k=k_k[None], k_a=k_a[None],
                r_k=r_k.reshape(1, D), ln_w=ln_w[None], ln_b=ln_b[None])


def _tile(n, pref):
    t = min(pref, n)
    while n % t:
        t //= 2
    return t


def attention_layer(x, B, S, g_pre, g_post, ap, bias_t, cos_t, sin_t, H):
    T, D = x.shape
    tm = _tile(T, 512)
    qr = ap["q_norm"].shape[1]
    n_lat = ap["w_lat"].shape[1]
    lat = fused_mm([x], [g_pre], ap["w_lat"], _norm_prologue, _store_epilogue, out_dtype=F32,
                   tm=tm, tn=n_lat, name="attn_latent")
    n_dil = ap["w_dil"].shape[1]
    dil_scale = jnp.concatenate([jnp.full((1, n_dil // 3), DIL_HEAD_DIM ** -0.5 * LOG2E, F32),
                                 jnp.ones((1, 2 * n_dil // 3), F32)], axis=1)
    qkv_b = fused_mm([x], [g_pre], ap["w_dil"], _norm_prologue, _scale_epilogue, out_dtype=BF16,
                     tm=tm, tn=n_dil // 3,
                     extras=[(dil_scale, pl.BlockSpec((1, n_dil // 3), lambda i, j: (0, j)))],
                     name="attn_dil_qkv")
    q_scale = (MLA_NOPE_DIM + MLA_ROPE_DIM) ** -0.5 * LOG2E
    nq = ap["w_q"].shape[1]
    tab = pl.BlockSpec((tm, LANES), lambda i, j: (i, 0))
    q_all = fused_mm([lat], [ap["q_norm"]], ap["w_q"], _norm_prologue,
                     functools.partial(_q_epilogue, scale=q_scale), out_dtype=BF16, tm=tm,
                     tn=nq // 2, row_cols=[0], row_width=qr, extras=[(cos_t, tab), (sin_t, tab)],
                     name="mla_q")
    kv = fused_mm([lat], [ap["kv_norm"]], ap["w_kv"], _norm_prologue, _store_epilogue,
                  out_dtype=BF16, tm=tm, tn=ap["w_kv"].shape[1] // 2, row_cols=[1], row_width=qr,
                  name="mla_kv")
    krope = rope_k(lat, (n_lat - LANES) // LANES, cos_t, sin_t, tm=tm)
    a_out = mla_attention(q_all, kv, krope, B, S, H, tq=MXU_DIM, n_chain=4,
                          tk=_tile(S // 2, 2048))
    b_out = dilated_attention(qkv_b, bias_t, B, S, H, t=bias_t.shape[-1], n_chain=4)
    return mm_post(a_out, b_out, ap["w_out"], g_post, x, tm=_tile(T, 256), name="attn_out")


def rwkv_layer(x, B, S, g_pre, g_post, rp):
    T, D = x.shape
    kb = MXU_DIM
    tn = 4 * kb
    assert D % tn == 0
    rkv = fused_mm([x], [g_pre, rp["mu"]], rp["w_proj"], _shift_prologue, _store_epilogue,
                   out_dtype=F32, tm=_tile(S, 512), tn=tn, n_mix=6, tiles_per_mix=D // tn,
                   seq_len=S, tail_mixes=(3, 4, 5, 5), tail_epilogue=_lora1_tail,
                   name="rwkv_proj")
    l2 = lora2(rkv, 3 * D, rp["w_l2"], rp["b_l2"], tm=_tile(T, 512), tn=tn, kb=kb)
    y_fwd, y_bwd = rwkv_scan(rkv, l2, rp["k_k"], rp["k_a"], B, S, D)
    return rwkv_out(y_fwd, y_bwd, rkv, l2, rp["r_k"], rp["ln_w"], rp["ln_b"], rp["w_o"], g_post, x,
                    tm=_tile(T, 256))


def trunk(x3, p):
    B, S, D = x3.shape
    x = x3.reshape(B * S, D)
    depth = p["norm_g"].shape[0]
    for layer in range(depth):
        i = layer // 2
        g = p["norm_g"][layer]
        if layer % 2 == 0:
            x = attention_layer(x, B, S, g[0][None], g[1][None], p["attn"][i], p["bias_t"],
                                p["cos_t"][(B, S)], p["sin_t"][(B, S)], p["heads"])
        else:
            x = rwkv_layer(x, B, S, g[0][None], g[1][None], p["rwkv"][i])
        x = ffn(x, g[2][None], g[3][None], p["ffn_wg"][layer], p["ffn_wu"][layer],
                p["ffn_wd"][layer], tm=_tile(B * S, 512), tf=_tile(p["ffn_wg"][layer].shape[1], 512))
    return x.reshape(B, S, D)


def kernel(x_prompt, x_sample, norm_g, rel_bias, at_w_in, at_q_norm, at_kv_norm, at_w_uq, at_w_ukv, at_w_out, rw_mu, rw_w_r, rw_w_k, rw_w_v, rw_w_o, rw_w0, rw_w1, rw_w2, rw_a0, rw_a1, rw_a2, rw_g1, rw_g2, rw_k_k, rw_k_a, rw_r_k, rw_ln_w, rw_ln_b, ffn_w_gate, ffn_w_up, ffn_w_down):
    H = rel_bias.shape[1]
    p = {"norm_g": norm_g, "heads": H}
    p["attn"] = [_attn_params(at_w_in[i], at_q_norm[i], at_kv_norm[i], at_w_uq[i], at_w_ukv[i],
                              at_w_out[i], H) for i in range(at_w_in.shape[0])]
    p["rwkv"] = [_rwkv_params(rw_mu[i], rw_w_r[i], rw_w_k[i], rw_w_v[i], rw_w_o[i], rw_w0[i],
                              rw_w1[i], rw_w2[i], rw_a0[i], rw_a1[i], rw_a2[i], rw_g1[i],
                              rw_g2[i], rw_k_k[i], rw_k_a[i], rw_r_k[i], rw_ln_w[i], rw_ln_b[i])
                 for i in range(rw_mu.shape[0])]
    p["ffn_wg"] = ffn_w_gate.astype(BF16)
    p["ffn_wu"] = ffn_w_up.astype(BF16)
    p["ffn_wd"] = ffn_w_down.astype(BF16)
    p["bias_t"] = dilated_bias_tiles(rel_bias, MXU_DIM)
    p["cos_t"], p["sin_t"] = {}, {}
    for xs in (x_prompt, x_sample):
        B, S = xs.shape[:2]
        p["cos_t"][(B, S)], p["sin_t"][(B, S)] = _rope_tables(B, S)
    return trunk(x_prompt, p), trunk(x_sample, p)
```

```python
import functools
import math

import jax
import jax.numpy as jnp
import numpy as np
from jax import lax
from jax.experimental import pallas as pl
from jax.experimental.pallas import tpu as pltpu

F32 = jnp.float32
BF16 = jnp.bfloat16

MLA_NOPE_DIM = 128
MLA_ROPE_DIM = 64
MLA_V_DIM = 128
ROPE_BASE = 10000.0
DIL_HEAD_DIM = 128
DIL_PATTERNS = ((128, 1), (512, 4), (2048, 16))
N_BUCKETS = 32
T5_MAX_DISTANCE = 1024
RWKV_HEAD = 64
RWKV_LN_EPS = 64e-5
NORM_EPS = 1e-6
NEG_BIG = -1e30
LOG2E = 1.4426950408889634

LANES = 128
SUBLANES = 8
MXU_DIM = 256
VMEM_LIMIT = 56 * 1024 * 1024

SCAN_CHUNK = 64
SCAN_LANES = 2048
SCAN_GROUP = 128


def _cparams(sem, vmem=VMEM_LIMIT):
    return pltpu.CompilerParams(dimension_semantics=sem, vmem_limit_bytes=vmem)


def _rms(x, g):
    return x * lax.rsqrt(jnp.mean(x * x, axis=-1, keepdims=True) + NORM_EPS) * g


def _dot(a, b):
    return jnp.dot(a, b, preferred_element_type=F32)


def _dot_nt(a, b):
    return lax.dot_general(a, b, (((1,), (1,)), ((), ())), preferred_element_type=F32)


def _dot_tn(a, b):
    return lax.dot_general(a, b, (((0,), (0,)), ((), ())), preferred_element_type=F32)


def _sigmoid(x):
    return 1.0 / (1.0 + jnp.exp(-x))


def _fused_mm_kernel(*refs, n_rows, n_vecs, n_extra, n_mix, tiles_per_mix, tiles_per_seq,
                     prologue, epilogue, tail_mixes, tail_epilogue):
    rows = refs[:n_rows]
    vecs = refs[n_rows:n_rows + n_vecs]
    w_ref = refs[n_rows + n_vecs]
    base = n_rows + n_vecs + 1
    extras = refs[base:base + n_extra]
    o_ref = refs[base + n_extra]
    h_ref = refs[base + n_extra + 1]
    j = pl.program_id(1)
    flags = ()
    if tiles_per_seq:
        it = pl.program_id(0) % tiles_per_seq
        flags = (jnp.where(it == 0, 0.0, 1.0), jnp.where(it == tiles_per_seq - 1, 0.0, 1.0))

    @pl.when(j == 0)
    def _():
        def emit(m, value):
            h_ref[m] = value.astype(BF16)

        prologue(emit, *[r[...] for r in rows], *flags, *[v[...] for v in vecs])

    if n_mix == 1:
        epilogue(j, _dot(h_ref[0], w_ref[...]), o_ref, *extras)
    elif not tail_mixes:
        epilogue(j, _dot(h_ref[j // tiles_per_mix], w_ref[...]), o_ref, *extras)
    else:
        last = pl.num_programs(1) - 1

        @pl.when(j < last)
        def _():
            epilogue(j, _dot(h_ref[j // tiles_per_mix], w_ref[...]), o_ref, *extras)

        @pl.when(j == last)
        def _():
            sub = w_ref.shape[1] // len(tail_mixes)
            tail_epilogue([_dot(h_ref[m], w_ref[:, s * sub:(s + 1) * sub])
                           for s, m in enumerate(tail_mixes)], o_ref)


def fused_mm(rows, vecs, w, prologue, epilogue, *, out_dtype, tm, tn, row_cols=None,
             row_width=None, n_mix=1, tiles_per_mix=1, extras=(), seq_len=None, tail_mixes=(),
             tail_epilogue=None, name):
    T = rows[0].shape[0]
    K, N = w.shape
    row_cols = row_cols or [0] * len(rows)
    row_width = row_width or K
    assert T % tm == 0 and N % tn == 0, (T, tm, N, tn)
    in_specs = [pl.BlockSpec((tm, row_width), functools.partial(lambda i, j, c: (i, c), c=c))
                for c in row_cols]
    if seq_len:
        assert len(rows) == 1 and seq_len % tm == 0
        g8, last8 = tm // SUBLANES, T // SUBLANES - 1
        rows = [rows[0]] * 3
        in_specs += [pl.BlockSpec((SUBLANES, K), lambda i, j: (jnp.maximum(i * g8 - 1, 0), 0)),
                     pl.BlockSpec((SUBLANES, K), lambda i, j: (jnp.minimum((i + 1) * g8, last8), 0))]
    in_specs += [pl.BlockSpec(v.shape, lambda i, j: (0, 0)) for v in vecs]
    in_specs += [pl.BlockSpec((K, tn), lambda i, j: (0, j))]
    in_specs += [spec for _, spec in extras]
    kern = functools.partial(
        _fused_mm_kernel, n_rows=len(rows), n_vecs=len(vecs), n_extra=len(extras), n_mix=n_mix,
        tiles_per_mix=tiles_per_mix, tiles_per_seq=seq_len // tm if seq_len else 0,
        prologue=prologue, epilogue=epilogue, tail_mixes=tuple(tail_mixes),
        tail_epilogue=tail_epilogue)
    return pl.pallas_call(
        kern,
        grid=(T // tm, N // tn),
        in_specs=in_specs,
        out_specs=pl.BlockSpec((tm, tn), lambda i, j: (i, j)),
        out_shape=jax.ShapeDtypeStruct((T, N), out_dtype),
        scratch_shapes=[pltpu.VMEM((n_mix, tm, K), BF16)],
        compiler_params=_cparams(("parallel", "arbitrary")),
        name=name,
    )(*rows, *vecs, w, *[a for a, _ in extras])


def _norm_prologue(emit, x, g):
    emit(0, _rms(x, g))


def _store_epilogue(j, acc, o_ref):
    o_ref[...] = acc.astype(o_ref.dtype)


def _scale_epilogue(j, acc, o_ref, s_ref):
    o_ref[...] = (acc * s_ref[...]).astype(o_ref.dtype)


def _mm_post_kernel(a_ref, b_ref, w_ref, g_ref, x_ref, o_ref):
    lhs = jnp.concatenate([a_ref[...], b_ref[...]], axis=-1)
    y = _dot(lhs, w_ref[...])
    o_ref[...] = x_ref[...] + _rms(y, g_ref[...])


def mm_post(a, b, w, g, x, *, tm, name):
    T, Ka = a.shape
    Kb = b.shape[1]
    D = w.shape[1]
    return pl.pallas_call(
        _mm_post_kernel,
        grid=(T // tm,),
        in_specs=[pl.BlockSpec((tm, Ka), lambda i: (i, 0)),
                  pl.BlockSpec((tm, Kb), lambda i: (i, 0)),
                  pl.BlockSpec((Ka + Kb, D), lambda i: (0, 0), pipeline_mode=pl.Buffered(1)),
                  pl.BlockSpec((1, D), lambda i: (0, 0)),
                  pl.BlockSpec((tm, D), lambda i: (i, 0))],
        out_specs=pl.BlockSpec((tm, D), lambda i: (i, 0)),
        out_shape=jax.ShapeDtypeStruct((T, D), F32),
        compiler_params=_cparams(("parallel",)),
        name=name,
    )(a, b, w, g, x)


def _ffn_kernel(x_ref, gpre_ref, gpost_ref, wg_ref, wu_ref, wd_ref, o_ref, h_ref, acc_ref):
    f = pl.program_id(1)

    @pl.when(f == 0)
    def _():
        h_ref[...] = _rms(x_ref[...], gpre_ref[...]).astype(BF16)
        acc_ref[...] = jnp.zeros_like(acc_ref)

    h = h_ref[...]
    gate = _dot(h, wg_ref[...])
    up = _dot(h, wu_ref[...])
    act = (gate * _sigmoid(gate) * up).astype(BF16)
    acc_ref[...] += _dot(act, wd_ref[...])

    @pl.when(f == pl.num_programs(1) - 1)
    def _():
        o_ref[...] = x_ref[...] + _rms(acc_ref[...], gpost_ref[...])


def ffn(x, g_pre, g_post, wg, wu, wd, *, tm, tf):
    T, D = x.shape
    Fh = wg.shape[1]
    assert T % tm == 0 and Fh % tf == 0
    return pl.pallas_call(
        _ffn_kernel,
        grid=(T // tm, Fh // tf),
        in_specs=[pl.BlockSpec((tm, D), lambda i, f: (i, 0)),
                  pl.BlockSpec((1, D), lambda i, f: (0, 0)),
                  pl.BlockSpec((1, D), lambda i, f: (0, 0)),
                  pl.BlockSpec((D, tf), lambda i, f: (0, f)),
                  pl.BlockSpec((D, tf), lambda i, f: (0, f)),
                  pl.BlockSpec((tf, D), lambda i, f: (f, 0))],
        out_specs=pl.BlockSpec((tm, D), lambda i, f: (i, 0)),
        out_shape=jax.ShapeDtypeStruct((T, D), F32),
        scratch_shapes=[pltpu.VMEM((tm, D), BF16), pltpu.VMEM((tm, D), F32)],
        compiler_params=_cparams(("parallel", "arbitrary")),
        name="ffn",
    )(x, g_pre, g_post, wg, wu, wd)


def _rope_groups(x, cos_t, sin_t):
    half = MLA_ROPE_DIM // 2
    n = x.shape[-1] // LANES
    lane = lax.broadcasted_iota(jnp.int32, x.shape, 1) % LANES
    partner = jnp.where(lane < half, pltpu.roll(x, x.shape[-1] - half, 1), pltpu.roll(x, half, 1))
    if n > 1:
        cos_t = jnp.concatenate([cos_t] * n, axis=-1)
        sin_t = jnp.concatenate([sin_t] * n, axis=-1)
    return x * cos_t + partner * sin_t


def _q_epilogue(j, acc, o_ref, cos_ref, sin_ref, *, scale):
    @pl.when(j == 0)
    def _():
        o_ref[...] = (acc * scale).astype(o_ref.dtype)

    @pl.when(j == 1)
    def _():
        o_ref[...] = (_rope_groups(acc, cos_ref[...], sin_ref[...]) * scale).astype(o_ref.dtype)


def _rope_k_kernel(x_ref, cos_ref, sin_ref, o_ref):
    o_ref[...] = _rope_groups(x_ref[...], cos_ref[...], sin_ref[...]).astype(o_ref.dtype)


def rope_k(lat, col_block, cos_t, sin_t, *, tm):
    T = lat.shape[0]
    return pl.pallas_call(
        _rope_k_kernel,
        grid=(T // tm,),
        in_specs=[pl.BlockSpec((tm, LANES), lambda i: (i, col_block)),
                  pl.BlockSpec((tm, LANES), lambda i: (i, 0)),
                  pl.BlockSpec((tm, LANES), lambda i: (i, 0))],
        out_specs=pl.BlockSpec((tm, LANES), lambda i: (i, 0)),
        out_shape=jax.ShapeDtypeStruct((T, LANES), BF16),
        compiler_params=_cparams(("parallel",)),
        name="rope_k",
    )(lat, cos_t, sin_t)


def _softmax_weights(ss, m_prev):
    m_new, ps = [], []
    for h, tiles in enumerate(ss):
        m = m_prev[h]
        for s in tiles:
            m = jnp.maximum(m, jnp.max(s, axis=0, keepdims=True))
        m_new.append(m)
    for h, tiles in enumerate(ss):
        ps.append([jnp.exp2(s - m_new[h]) for s in tiles])
    return m_new, ps


def _weighted_values(ps, vs, l, acc):
    for p, v in zip(ps, vs):
        l = l + jnp.sum(p, axis=0, keepdims=True)
        acc = acc + _dot_tn(v, p.astype(BF16))
    return l, acc


def _mla_kernel(qn_ref, qr_ref, kn_ref, kr_ref, v_ref, o_ref, m_ref, l_ref, acc_ref, sa_ref,
                sb_ref, *, tk):
    nkv = kn_ref.shape[0] // tk
    assert nkv % 2 == 0
    n, _, tq = acc_ref.shape
    q = jnp.concatenate([qn_ref[...], qr_ref[...]], axis=-1)
    qs = [q[h * tq:(h + 1) * tq] for h in range(n)]
    m_ref[...] = jnp.full_like(m_ref, NEG_BIG)
    l_ref[...] = jnp.zeros_like(l_ref)
    acc_ref[...] = jnp.zeros_like(acc_ref)

    def chunk(c):
        return pl.ds(c * tk if isinstance(c, int) else pl.multiple_of(c * tk, tk), tk)

    def put_scores(c, s_ref):
        k = jnp.concatenate([kn_ref[chunk(c), :], kr_ref[chunk(c), :]], axis=-1)
        for h in range(n):
            s_ref[h] = _dot_nt(k, qs[h])

    def consume(c, s_ref):
        v = v_ref[chunk(c), :]
        m_prev = [m_ref[h] for h in range(n)]
        m_new, ps = _softmax_weights([[s_ref[h]] for h in range(n)], m_prev)
        for h in range(n):
            alpha = jnp.exp2(m_prev[h] - m_new[h])
            l, acc = _weighted_values(ps[h], [v], alpha * l_ref[h], alpha * acc_ref[h])
            l_ref[h] = l
            acc_ref[h] = acc
            m_ref[h] = m_new[h]

    put_scores(0, sa_ref)

    def body(i, carry):
        c = 2 * i
        put_scores(c + 1, sb_ref)
        consume(c, sa_ref)
        put_scores(c + 2, sa_ref)
        consume(c + 1, sb_ref)
        return carry

    lax.fori_loop(0, nkv // 2 - 1, body, 0)
    put_scores(nkv - 1, sb_ref)
    consume(nkv - 2, sa_ref)
    consume(nkv - 1, sb_ref)
    for h in range(n):
        o_ref[h * tq:(h + 1) * tq, :] = (acc_ref[h] / l_ref[h]).T.astype(o_ref.dtype)


def mla_attention(q_all, kv, krope, B, S, H, *, tq, n_chain, tk):
    T = B * S
    tb = tq * n_chain
    nq = S // tb
    dv = MLA_V_DIM
    return pl.pallas_call(
        functools.partial(_mla_kernel, tk=tk),
        grid=(B, H, nq),
        in_specs=[pl.BlockSpec((tb, LANES), lambda b, h, i: (b * nq + i, h)),
                  pl.BlockSpec((tb, LANES), lambda b, h, i: (b * nq + i, H + h)),
                  pl.BlockSpec((S, LANES), lambda b, h, i: (b, h)),
                  pl.BlockSpec((S, LANES), lambda b, h, i: (b, 0)),
                  pl.BlockSpec((S, LANES), lambda b, h, i: (b, H + h))],
        out_specs=pl.BlockSpec((tb, dv), lambda b, h, i: (b * nq + i, h)),
        out_shape=jax.ShapeDtypeStruct((T, H * dv), BF16),
        scratch_shapes=[pltpu.VMEM((n_chain, 1, tq), F32), pltpu.VMEM((n_chain, 1, tq), F32),
                        pltpu.VMEM((n_chain, dv, tq), F32), pltpu.VMEM((n_chain, tk, tq), F32),
                        pltpu.VMEM((n_chain, tk, tq), F32)],
        compiler_params=_cparams(("parallel", "parallel", "arbitrary")),
        name="mla_attention",
    )(q_all, q_all, kv, krope, kv)


def _dil_kernel(q_ref, k_ref, v_ref, bias_ref, o_ref, *, t, nside, n_chain):
    nk = k_ref.shape[0] // t
    i = pl.program_id(2)
    ss, vs = [], []
    for h in range(n_chain):
        q = q_ref[h * t:(h + 1) * t, :]
        tiles, vals = [], []
        for d in range(2 * nside + 1):
            c = i * n_chain + h + d - nside
            inside = jnp.logical_and(c >= 0, c < nk)
            rows = pl.ds(pl.multiple_of(jnp.clip(c, 0, nk - 1) * t, t), t)
            tiles.append(_dot_nt(k_ref[rows, :], q)
                         + (bias_ref[d] + jnp.where(inside, 0.0, NEG_BIG)))
            vals.append(v_ref[rows, :])
        ss.append(tiles)
        vs.append(vals)
    m_new, ps = _softmax_weights(ss, [jnp.full((1, t), NEG_BIG, F32)] * n_chain)
    for h in range(n_chain):
        l, acc = _weighted_values(ps[h], vs[h], jnp.zeros((1, t), F32),
                                  jnp.zeros((DIL_HEAD_DIM, t), F32))
        o_ref[h * t:(h + 1) * t, :] = (acc / l).T.astype(o_ref.dtype)


def dilated_attention(qkv, bias_t, B, S, H, *, t, n_chain):
    T = B * S
    tb = t * n_chain
    nq = S // tb
    nd = bias_t.shape[0]
    dh = DIL_HEAD_DIM
    return pl.pallas_call(
        functools.partial(_dil_kernel, t=t, nside=(nd - 1) // 2, n_chain=n_chain),
        grid=(B, H, nq),
        in_specs=[pl.BlockSpec((tb, dh), lambda b, h, i: (b * nq + i, h)),
                  pl.BlockSpec((S, dh), lambda b, h, i: (b, H + h)),
                  pl.BlockSpec((S, dh), lambda b, h, i: (b, 2 * H + h)),
                  pl.BlockSpec((nd, None, t, t), lambda b, h, i: (0, h, 0, 0))],
        out_specs=pl.BlockSpec((tb, dh), lambda b, h, i: (b * nq + i, h)),
        out_shape=jax.ShapeDtypeStruct((T, H * dh), BF16),
        compiler_params=_cparams(("parallel", "parallel", "arbitrary")),
        name="dilated_attention",
    )(qkv, qkv, qkv, bias_t)


def _t5_bucket_np(rel):
    half = N_BUCKETS // 2
    max_exact = half // 2
    bucket = np.where(rel > 0, half, 0)
    n = np.abs(rel)
    nf = np.maximum(n, 1).astype(np.float64)
    large = max_exact + (np.log(nf / max_exact) / math.log(T5_MAX_DISTANCE / max_exact)
                         * (half - max_exact)).astype(np.int64)
    large = np.minimum(large, half - 1)
    return bucket + np.where(n < max_exact, n, large)


def dilated_bias_tiles(rel_bias, t):
    reach = max(w // 2 for w, _ in DIL_PATTERNS)
    nside = -(-reach // t)
    nd = 2 * nside + 1
    span = (nside + 1) * t
    delta = np.arange(-span + 1, span)
    mult = np.zeros(delta.shape, np.int64)
    for window, dil in DIL_PATTERNS:
        mult += ((delta % dil == 0) & (np.abs(delta) <= window // 2)).astype(np.int64)
    logm = np.log(np.maximum(mult, 1)).astype(np.float32)
    f = (rel_bias.astype(F32)[_t5_bucket_np(delta)] + logm[:, None]) * LOG2E
    f = jnp.where((mult > 0)[:, None], f, NEG_BIG)
    H = f.shape[1]
    frev = jnp.pad(f[::-1].T, ((0, 0), (1, 0)))
    wins = []
    for d in range(nd):
        s_d = span - 1 - (d - nside) * t + 1
        wins.append(jnp.concatenate([frev[:, s_d:s_d + t], frev[:, s_d - t:s_d]], axis=1))
    g = jnp.stack(wins, axis=0)
    flat = jnp.tile(g, (1, 1, t))[:, :, :t * (2 * t - 1)]
    return flat.reshape(nd, H, t, 2 * t - 1)[:, :, :, :t]


def _shift_prologue(emit, x, x_before, x_after, keep_before, keep_after, g, mu):
    tm = x.shape[0]
    h = _rms(x, g)
    edge_prev = _rms(x_before[SUBLANES - 1:SUBLANES, :], g) * keep_before
    edge_next = _rms(x_after[0:1, :], g) * keep_after
    row = lax.broadcasted_iota(jnp.int32, h.shape, 0)
    h_prev = jnp.where(row == 0, edge_prev, pltpu.roll(h, 1, 0))
    h_next = jnp.where(row == tm - 1, edge_next, pltpu.roll(h, tm - 1, 0))
    h_b = h.astype(BF16)
    xx_b = (0.5 * (h_prev + h_next) - h).astype(BF16)
    mu_b = mu.astype(BF16)
    for m in range(mu.shape[0]):
        emit(m, h_b + xx_b * mu_b[m:m + 1, :])


def _lora1_tail(accs, o_ref):
    sub = accs[0].shape[1]
    acts = (_sigmoid(accs[0]), jnp.tanh(accs[1]), accs[2], jnp.zeros_like(accs[3]))
    for s, a in enumerate(acts):
        o_ref[:, s * sub:(s + 1) * sub] = a.astype(o_ref.dtype)


def _lora2_kernel(a_ref, w_ref, b_ref, o_ref, *, tiles_per_d):
    j = pl.program_id(1)
    z = _dot(a_ref[...].astype(BF16), w_ref[...]) + b_ref[...]

    @pl.when(j < tiles_per_d)
    def _():
        o_ref[...] = z

    @pl.when(jnp.logical_and(j >= tiles_per_d, j < 3 * tiles_per_d))
    def _():
        o_ref[...] = -math.exp(-0.5) * _sigmoid(z)

    @pl.when(j >= 3 * tiles_per_d)
    def _():
        o_ref[...] = _sigmoid(z)


def lora2(l1, col0, w, bias, *, tm, tn, kb):
    T = l1.shape[0]
    N = w.shape[1]
    D = N // 5
    assert D % tn == 0 and col0 % kb == 0

    def a_map(i, j):
        return (i, col0 // kb + (j * tn + D) // (2 * D))

    return pl.pallas_call(
        functools.partial(_lora2_kernel, tiles_per_d=D // tn),
        grid=(T // tm, N // tn),
        in_specs=[pl.BlockSpec((tm, kb), a_map), pl.BlockSpec((kb, tn), lambda i, j: (0, j)),
                  pl.BlockSpec((1, tn), lambda i, j: (0, j))],
        out_specs=pl.BlockSpec((tm, tn), lambda i, j: (i, j)),
        out_shape=jax.ShapeDtypeStruct((T, N), F32),
        compiler_params=_cparams(("parallel", "parallel")),
        name="rwkv_lora2",
    )(l1, w, bias)


def _head_block_mask(n):
    r = lax.broadcasted_iota(jnp.int32, (n, n), 0) // RWKV_HEAD
    c = lax.broadcasted_iota(jnp.int32, (n, n), 1) // RWKV_HEAD
    return r == c


def _head_sums(x):
    ones = jnp.where(_head_block_mask(MXU_DIM), 1.0, 0.0).astype(BF16)
    parts = [_dot(x[:, g:g + MXU_DIM].astype(BF16), ones) for g in range(0, x.shape[-1], MXU_DIM)]
    return parts[0] if len(parts) == 1 else jnp.concatenate(parts, axis=-1)


def _split3(x):
    hi = x.astype(BF16)
    r1 = x - hi.astype(F32)
    mid = r1.astype(BF16)
    lo = (r1 - mid.astype(F32)).astype(BF16)
    return hi, mid, lo


SCAN_OPERANDS = ("at", "rt", "bt", "kt", "at_s", "rt_s", "bh", "kh", "v")


def _scan_prepare(r_ref, k_ref, v_ref, lw_ref, a_ref, kk_ref, ka_ref, sign):
    C = SCAN_CHUNK
    r = r_ref[...]
    k = k_ref[...]
    kk = k * kk_ref[...]
    kk = kk * lax.rsqrt(_head_sums(kk * kk) + 1e-12)
    lw = lw_ref[...]
    a = a_ref[...]
    kd = k * (1.0 + (a - 1.0) * ka_ref[...])
    aa = -kk
    bb = kk * a

    trow = lax.broadcasted_iota(jnp.int32, (C, C), 0)
    tcol = lax.broadcasted_iota(jnp.int32, (C, C), 1)
    cum = jnp.where((trow - tcol) * sign >= 0, 1.0, 0.0).astype(BF16)
    hi, mid, lo = _split3(lw)
    l_inc = _dot(cum, hi) + _dot(cum, mid) + _dot(cum, lo)
    l_exc = l_inc - lw
    l_ref = l_inc[C // 2:C // 2 + 1, :]
    l_end = l_inc[C - 1:C, :] if sign > 0 else l_inc[0:1, :]
    e_in = jnp.exp(l_inc - l_ref)
    e_ex = jnp.exp(l_exc - l_ref)
    e_ng = jnp.exp(l_ref - l_inc)
    rho = jnp.exp(l_ref)
    end_over_ref = jnp.exp(l_end - l_ref)
    at = aa * e_ex
    rt = r * e_in
    bt = bb * e_ng
    kt = kd * e_ng
    vals = dict(at=at, rt=rt, bt=bt, kt=kt, at_s=at * rho, rt_s=rt * rho, bh=bt * end_over_ref,
                kh=kt * end_over_ref, v=v_ref[...])
    return {name: vals[name].astype(BF16) for name in SCAN_OPERANDS}, jnp.exp(l_end)


def _scan_advance(chains):
    C = SCAN_CHUNK
    HN = RWKV_HEAD
    GW = chains[0]["at"].shape[-1]
    cat = jnp.concatenate
    lane_head = lax.broadcasted_iota(jnp.int32, (1, GW), 1) // HN

    def mstack(x):
        zero = jnp.zeros_like(x)
        return cat([jnp.where(lane_head == j, x, zero) for j in range(GW // HN)], axis=0)

    trow = lax.broadcasted_iota(jnp.int32, (C, GW), 0)
    scol = lax.broadcasted_iota(jnp.int32, (C, GW), 1) % HN
    eye = jnp.where(scol == trow, 1.0, 0.0)
    bdiag = _head_block_mask(GW)
    masks = {}
    for sign in sorted({ch["sign"] for ch in chains}):
        couples = [jnp.logical_and(((trow >> lvl) & 1) - ((scol >> lvl) & 1) == sign,
                                   (trow >> (lvl + 1)) == (scol >> (lvl + 1)))
                   for lvl in range(int(math.log2(C)))]
        masks[sign] = dict(strict=(trow - scol) * sign > 0, incl=(trow - scol) * sign >= 0,
                           couples=couples)
    strict = [masks[ch["sign"]]["strict"] for ch in chains]
    incl = [masks[ch["sign"]]["incl"] for ch in chains]

    sc = [_dot_nt(cat([ch["at"], ch["rt"]], axis=0),
                  cat([mstack(ch["bt"]), mstack(ch["kt"])], axis=0)) for ch in chains]
    a_ab = [jnp.where(m, s[:C, :GW], 0.0) for m, s in zip(strict, sc)]
    a_ak = [jnp.where(m, s[:C, GW:], 0.0) for m, s in zip(strict, sc)]
    a_r = [cat([jnp.where(m, s[C:, :GW], 0.0), jnp.where(m, s[C:, GW:], 0.0)], axis=1).astype(BF16)
           for m, s in zip(incl, sc)]

    av = [_dot(x.astype(BF16), mstack(ch["v"])) for x, ch in zip(a_ak, chains)]
    tinv = None
    for lvl in range(int(math.log2(C))):
        low = [jnp.where(masks[ch["sign"]]["couples"][lvl], x, 0.0) for ch, x in zip(chains, a_ab)]
        if tinv is None:
            tinv = [eye + x for x in low]
            continue
        ld = [_dot(x.astype(BF16), mstack(t.astype(BF16))) for x, t in zip(low, tinv)]
        tinv = [t + _dot(t.astype(BF16), mstack(x.astype(BF16))) for t, x in zip(tinv, ld)]

    uw = [_dot(t.astype(BF16), cat([mstack(x.astype(BF16)), mstack(ch["at_s"])], axis=1))
          for t, x, ch in zip(tinv, av, chains)]

    s_old = [ch["s_ref"][ch["slot"]] for ch in chains]
    us = [_dot_nt(cat([w[:, GW:].astype(BF16), ch["rt_s"]], axis=0), s.astype(BF16))
          for w, ch, s in zip(uw, chains, s_old)]
    u = [x[:C] + w[:, :GW] for x, w in zip(us, uw)]
    y = [x[C:] + _dot(ar, cat([mstack(uu.astype(BF16)), mstack(ch["v"])], axis=0))
         for x, ar, uu, ch in zip(us, a_r, u, chains)]
    for ch, yy in zip(chains, y):
        ch["y_ref"][:, ch["lanes"]] = yy

    s_new = [_dot(cat([uu, ch["v"].astype(F32)], axis=0).T.astype(BF16),
                  cat([ch["bh"], ch["kh"]], axis=0)) for uu, ch in zip(u, chains)]
    for ch, sn, so in zip(chains, s_new, s_old):
        ch["s_ref"][ch["slot"]] = jnp.where(bdiag, sn, 0.0) + so * ch["p_end"]


def _scan_kernel(*refs):
    fwd_in, bwd_in = refs[0:5], refs[5:10]
    kk_ref, ka_ref, yf_ref, yb_ref, s_ref = refs[10:15]

    @pl.when(pl.program_id(2) == 0)
    def _():
        s_ref[...] = jnp.zeros_like(s_ref)

    L = kk_ref.shape[-1]
    gw = min(SCAN_GROUP, L)
    chains = []
    for di, (ins, y_ref, sign) in enumerate(((fwd_in, yf_ref, 1), (bwd_in, yb_ref, -1))):
        ops, p_end = _scan_prepare(*ins, kk_ref, ka_ref, sign)
        for p in range(L // gw):
            lanes = slice(p * gw, (p + 1) * gw)
            ch = {name: ops[name][:, lanes] for name in SCAN_OPERANDS}
            ch.update(sign=sign, p_end=p_end[:, lanes], lanes=lanes, y_ref=y_ref, s_ref=s_ref,
                      slot=di * (L // gw) + p)
            chains.append(ch)
    _scan_advance(chains)


def rwkv_scan(rkv, l2, k_k, k_a, B, S, D):
    T = B * S
    C = SCAN_CHUNK
    L = min(SCAN_LANES, D)
    gw = min(SCAN_GROUP, L)
    nc = S // C
    ng = D // L

    def row(b, c, d):
        return b * nc + (nc - 1 - c if d else c)

    def spec(col, d):
        return pl.BlockSpec((C, L), lambda b, g, c: (row(b, c, d), col * ng + g))

    def dir_specs(d):
        return [spec(0, d), spec(1, d), spec(2, d), spec(1 + d, d), spec(3 + d, d)]

    vec = pl.BlockSpec((1, L), lambda b, g, c: (0, g))
    out = jax.ShapeDtypeStruct((T, D), F32)
    return pl.pallas_call(
        _scan_kernel,
        grid=(B, ng, nc),
        in_specs=dir_specs(0) + dir_specs(1) + [vec, vec],
        out_specs=[pl.BlockSpec((C, L), lambda b, g, c: (row(b, c, 0), g)),
                   pl.BlockSpec((C, L), lambda b, g, c: (row(b, c, 1), g))],
        out_shape=[out, out],
        scratch_shapes=[pltpu.VMEM((2 * L // gw, gw, gw), F32)],
        compiler_params=_cparams(("parallel", "parallel", "arbitrary")),
        name="rwkv_scan",
    )(rkv, rkv, rkv, l2, l2, rkv, rkv, rkv, l2, l2, k_k, k_a)


def _rwkv_out_kernel(yf_ref, yb_ref, r_ref, k_ref, v_ref, g_ref, rk_ref, lnw_ref, lnb_ref, wo_ref,
                     gpost_ref, x_ref, o_ref):
    inv_n = 1.0 / RWKV_HEAD
    y = yf_ref[...] + yb_ref[...]
    mean = _head_sums(y) * inv_n
    yc = y - mean
    var = _head_sums(yc * yc) * inv_n
    yn = yc * lax.rsqrt(var + RWKV_LN_EPS) * lnw_ref[...] + lnb_ref[...]
    bonus = _head_sums(r_ref[...] * k_ref[...] * rk_ref[...]) * v_ref[...]
    mix = ((yn + bonus) * g_ref[...]).astype(BF16)
    out = _dot(mix, wo_ref[...])
    o_ref[...] = x_ref[...] + _rms(out, gpost_ref[...])


def rwkv_out(y_fwd, y_bwd, rkv, l2, r_k, ln_w, ln_b, w_o, g_post, x, *, tm):
    T, D = x.shape
    vec = pl.BlockSpec((1, D), lambda i: (0, 0))
    return pl.pallas_call(
        _rwkv_out_kernel,
        grid=(T // tm,),
        in_specs=[pl.BlockSpec((tm, D), lambda i: (i, 0)),
                  pl.BlockSpec((tm, D), lambda i: (i, 0)),
                  pl.BlockSpec((tm, D), lambda i: (i, 0)),
                  pl.BlockSpec((tm, D), lambda i: (i, 1)),
                  pl.BlockSpec((tm, D), lambda i: (i, 2)),
                  pl.BlockSpec((tm, D), lambda i: (i, 0)),
                  vec, vec, vec,
                  pl.BlockSpec((D, D), lambda i: (0, 0), pipeline_mode=pl.Buffered(1)),
                  vec,
                  pl.BlockSpec((tm, D), lambda i: (i, 0))],
        out_specs=pl.BlockSpec((tm, D), lambda i: (i, 0)),
        out_shape=jax.ShapeDtypeStruct((T, D), F32),
        compiler_params=_cparams(("parallel",)),
        name="rwkv_out",
    )(y_fwd, y_bwd, rkv, rkv, rkv, l2, r_k, ln_w, ln_b, w_o, g_post, x)


def _pad_cols(w, n):
    return jnp.pad(w, ((0, 0), (0, n - w.shape[1])))


def _pad_rows(w, n):
    return jnp.pad(w, ((0, n - w.shape[0]), (0, 0)))


def _attn_params(w_in, q_norm, kv_norm, w_uq, w_ukv, w_out, H):
    qr, kvr = q_norm.shape[0], kv_norm.shape[0]
    rope, nope, vd = MLA_ROPE_DIM, MLA_NOPE_DIM, MLA_V_DIM
    w_lat = jnp.concatenate([w_in[:, :qr + kvr], _pad_cols(w_in[:, qr + kvr:qr + kvr + rope], LANES)],
                            axis=1).astype(BF16)
    w_dil = w_in[:, qr + kvr + rope:].astype(BF16)
    uq = w_uq.reshape(qr, H, nope + rope)
    uq_rope = jnp.pad(uq[:, :, nope:], ((0, 0), (0, 0), (0, LANES - rope)))
    w_q = jnp.concatenate([uq[:, :, :nope].reshape(qr, H * nope), uq_rope.reshape(qr, H * LANES)],
                          axis=1).astype(BF16)
    ukv = w_ukv.reshape(kvr, H, nope + vd)
    w_kv = jnp.concatenate([ukv[:, :, :nope].reshape(kvr, H * nope),
                            ukv[:, :, nope:].reshape(kvr, H * vd)], axis=1).astype(BF16)
    return dict(w_lat=w_lat, w_dil=w_dil, w_q=w_q, w_kv=w_kv, w_out=w_out.astype(BF16),
                q_norm=q_norm[None], kv_norm=kv_norm[None])


def _rope_tables(B, S):
    half = MLA_ROPE_DIM // 2
    inv = ROPE_BASE ** (-jnp.arange(0, MLA_ROPE_DIM, 2, dtype=F32) / MLA_ROPE_DIM)
    ang = jnp.arange(S, dtype=F32)[:, None] * inv[None, :]
    cos, sin = jnp.cos(ang), jnp.sin(ang)
    zeros = jnp.zeros((S, LANES - 2 * half), F32)
    cos_t = jnp.concatenate([cos, cos, zeros], axis=1)
    sin_t = jnp.concatenate([-sin, sin, zeros], axis=1)
    return jnp.tile(cos_t, (B, 1)), jnp.tile(sin_t, (B, 1))


def _rwkv_params(mu, w_r, w_k, w_v, w_o, w0, w1, w2, a0, a1, a2, g1, g2, k_k, k_a, r_k, ln_w,
                 ln_b):
    D = w_r.shape[0]
    kb = MXU_DIM
    w_l1 = jnp.concatenate([_pad_cols(g1, kb), _pad_cols(jnp.concatenate([w1[0], w1[1]], 1), kb),
                            _pad_cols(jnp.concatenate([a1[0], a1[1]], 1), kb),
                            jnp.zeros((D, kb), F32)], axis=1)
    w_proj = jnp.concatenate([w_r, w_k, w_v, w_l1], axis=1).astype(BF16)
    rw = w2.shape[1]
    ra = a2.shape[1]
    z = lambda n: jnp.zeros((n, D), F32)
    w_l2 = jnp.concatenate([
        _pad_rows(g2, kb),
        _pad_rows(w2[0], kb),
        _pad_rows(jnp.concatenate([z(rw), w2[1]], 0), kb),
        _pad_rows(a2[0], kb),
        _pad_rows(jnp.concatenate([z(ra), a2[1]], 0), kb)], axis=1).astype(BF16)
    return dict(w_proj=w_proj, w_l2=w_l2, w_o=w_o.astype(BF16),
                mu=jnp.stack([mu[0], mu[2], mu[3], mu[5], mu[1], mu[4]]),
                b_l2=jnp.concatenate([jnp.zeros((D,), F32), w0[0], w0[1], a0[0], a0[1]])[None],
                k_k=k_k[None], k_a=k_a[None],
                r_k=r_k.reshape(1, D), ln_w=ln_w[None], ln_b=ln_b[None])


def _tile(n, pref):
    t = min(pref, n)
    while n % t:
        t //= 2
    return t


def attention_layer(x, B, S, g_pre, g_post, ap, bias_t, cos_t, sin_t, H):
    T, D = x.shape
    tm = _tile(T, 512)
    qr = ap["q_norm"].shape[1]
    n_lat = ap["w_lat"].shape[1]
    lat = fused_mm([x], [g_pre], ap["w_lat"], _norm_prologue, _store_epilogue, out_dtype=F32,
                   tm=tm, tn=n_lat, name="attn_latent")
    n_dil = ap["w_dil"].shape[1]
    dil_scale = jnp.concatenate([jnp.full((1, n_dil // 3), DIL_HEAD_DIM ** -0.5 * LOG2E, F32),
                                 jnp.ones((1, 2 * n_dil // 3), F32)], axis=1)
    qkv_b = fused_mm([x], [g_pre], ap["w_dil"], _norm_prologue, _scale_epilogue, out_dtype=BF16,
                     tm=tm, tn=n_dil // 3,
                     extras=[(dil_scale, pl.BlockSpec((1, n_dil // 3), lambda i, j: (0, j)))],
                     name="attn_dil_qkv")
    q_scale = (MLA_NOPE_DIM + MLA_ROPE_DIM) ** -0.5 * LOG2E
    nq = ap["w_q"].shape[1]
    tab = pl.BlockSpec((tm, LANES), lambda i, j: (i, 0))
    q_all = fused_mm([lat], [ap["q_norm"]], ap["w_q"], _norm_prologue,
                     functools.partial(_q_epilogue, scale=q_scale), out_dtype=BF16, tm=tm,
                     tn=nq // 2, row_cols=[0], row_width=qr, extras=[(cos_t, tab), (sin_t, tab)],
                     name="mla_q")
    kv = fused_mm([lat], [ap["kv_norm"]], ap["w_kv"], _norm_prologue, _store_epilogue,
                  out_dtype=BF16, tm=tm, tn=ap["w_kv"].shape[1] // 2, row_cols=[1], row_width=qr,
                  name="mla_kv")
    krope = rope_k(lat, (n_lat - LANES) // LANES, cos_t, sin_t, tm=tm)
    a_out = mla_attention(q_all, kv, krope, B, S, H, tq=MXU_DIM, n_chain=4,
                          tk=_tile(S // 2, 2048))
    b_out = dilated_attention(qkv_b, bias_t, B, S, H, t=bias_t.shape[-1], n_chain=4)
    return mm_post(a_out, b_out, ap["w_out"], g_post, x, tm=_tile(T, 256), name="attn_out")


def rwkv_layer(x, B, S, g_pre, g_post, rp):
    T, D = x.shape
    kb = MXU_DIM
    tn = 4 * kb
    assert D % tn == 0
    rkv = fused_mm([x], [g_pre, rp["mu"]], rp["w_proj"], _shift_prologue, _store_epilogue,
                   out_dtype=F32, tm=_tile(S, 512), tn=tn, n_mix=6, tiles_per_mix=D // tn,
                   seq_len=S, tail_mixes=(3, 4, 5, 5), tail_epilogue=_lora1_tail,
                   name="rwkv_proj")
    l2 = lora2(rkv, 3 * D, rp["w_l2"], rp["b_l2"], tm=_tile(T, 512), tn=tn, kb=kb)
    y_fwd, y_bwd = rwkv_scan(rkv, l2, rp["k_k"], rp["k_a"], B, S, D)
    return rwkv_out(y_fwd, y_bwd, rkv, l2, rp["r_k"], rp["ln_w"], rp["ln_b"], rp["w_o"], g_post, x,
                    tm=_tile(T, 256))


def trunk(x3, p):
    B, S, D = x3.shape
    x = x3.reshape(B * S, D)
    depth = p["norm_g"].shape[0]
    for layer in range(depth):
        i = layer // 2
        g = p["norm_g"][layer]
        if layer % 2 == 0:
            x = attention_layer(x, B, S, g[0][None], g[1][None], p["attn"][i], p["bias_t"],
                                p["cos_t"][(B, S)], p["sin_t"][(B, S)], p["heads"])
        else:
            x = rwkv_layer(x, B, S, g[0][None], g[1][None], p["rwkv"][i])
        x = ffn(x, g[2][None], g[3][None], p["ffn_wg"][layer], p["ffn_wu"][layer],
                p["ffn_wd"][layer], tm=_tile(B * S, 512), tf=_tile(p["ffn_wg"][layer].shape[1], 512))
    return x.reshape(B, S, D)


def kernel(x_prompt, x_sample, norm_g, rel_bias, at_w_in, at_q_norm, at_kv_norm, at_w_uq, at_w_ukv, at_w_out, rw_mu, rw_w_r, rw_w_k, rw_w_v, rw_w_o, rw_w0, rw_w1, rw_w2, rw_a0, rw_a1, rw_a2, rw_g1, rw_g2, rw_k_k, rw_k_a, rw_r_k, rw_ln_w, rw_ln_b, ffn_w_gate, ffn_w_up, ffn_w_down):
    H = rel_bias.shape[1]
    p = {"norm_g": norm_g, "heads": H}
    p["attn"] = [_attn_params(at_w_in[i], at_q_norm[i], at_kv_norm[i], at_w_uq[i], at_w_ukv[i],
                              at_w_out[i], H) for i in range(at_w_in.shape[0])]
    p["rwkv"] = [_rwkv_params(rw_mu[i], rw_w_r[i], rw_w_k[i], rw_w_v[i], rw_w_o[i], rw_w0[i],
                              rw_w1[i], rw_w2[i], rw_a0[i], rw_a1[i], rw_a2[i], rw_g1[i],
                              rw_g2[i], rw_k_k[i], rw_k_a[i], rw_r_k[i], rw_ln_w[i], rw_ln_b[i])
                 for i in range(rw_mu.shape[0])]
    p["ffn_wg"] = ffn_w_gate.astype(BF16)
    p["ffn_wu"] = ffn_w_up.astype(BF16)
    p["ffn_wd"] = ffn_w_down.astype(BF16)
    p["bias_t"] = dilated_bias_tiles(rel_bias, MXU_DIM)
    p["cos_t"], p["sin_t"] = {}, {}
    for xs in (x_prompt, x_sample):
        B, S = xs.shape[:2]
        p["cos_t"][(B, S)], p["sin_t"][(B, S)] = _rope_tables(B, S)
    return trunk(x_prompt, p), trunk(x_sample, p)
```
